```python
import math
import jax, jax.numpy as jnp
from jax import lax
import numpy as np

D_MODEL = 1024
BATCH = 4
SEQ = 8192
DEPTH = 2

N_MIXERS = 2
N_ATTN_LAYERS = (DEPTH + N_MIXERS - 1) // N_MIXERS
N_RG_LAYERS = DEPTH // N_MIXERS

N_HEADS = 16
HEAD_DIM = D_MODEL // N_HEADS
Q_BLOCK = 128

D_RNN = 1344
RG_BLOCKS = 16
RG_BLOCK_W = D_RNN // RG_BLOCKS
RG_CONV_W = 4
RG_C = 8.0

D_FF = 2816
FFN_CONV_W = 3

NORM_EPS = 1e-6

kernel_name = "hybrid_stickbreak_rglru_convffn"


def rms_norm(x, g):
    xf = x.astype(jnp.float32)
    inv = lax.rsqrt(jnp.mean(xf * xf, axis=-1, keepdims=True) + NORM_EPS)
    return (xf * inv * g.astype(jnp.float32)).astype(x.dtype)


def causal_depthwise_conv(x, w, b):
    k = w.shape[0]
    out = lax.conv_general_dilated(
        x, w[:, None, :].astype(x.dtype), window_strides=(1,),
        padding=[(k - 1, 0)], dimension_numbers=("NWC", "WIO", "NWC"),
        feature_group_count=x.shape[-1])
    return out + b.astype(x.dtype)


def stick_breaking_attention(q, k, v):
    seq = q.shape[2]
    scale = HEAD_DIM ** -0.5
    outs = []
    for qb in range(seq // Q_BLOCK):
        q0 = qb * Q_BLOCK
        kend = q0 + Q_BLOCK
        z = jnp.einsum("bhtd,bhsd->bhts", q[:, :, q0:kend], k[:, :, :kend]).astype(jnp.float32) * scale
        t_pos = q0 + jnp.arange(Q_BLOCK)[:, None]
        s_pos = jnp.arange(kend)[None, :]
        causal = s_pos < t_pos
        log_beta = jax.nn.log_sigmoid(z)
        log_1m_beta = jnp.where(causal, jax.nn.log_sigmoid(-z), 0.0)
        log_stick = lax.cumsum(log_1m_beta, axis=3, reverse=True) - log_1m_beta
        w = jnp.where(causal, jnp.exp(log_beta + log_stick), 0.0)
        outs.append(jnp.einsum("bhts,bhsd->bhtd", w.astype(v.dtype), v[:, :, :kend]))
    return jnp.concatenate(outs, axis=2)


def attention_mixer(h, w_qkv, w_o):
    b, s, _ = h.shape
    qkv = (h @ w_qkv).reshape(b, s, 3, N_HEADS, HEAD_DIM)
    qkv = jnp.transpose(qkv, (2, 0, 3, 1, 4))
    o = stick_breaking_attention(qkv[0], qkv[1], qkv[2])
    o = jnp.transpose(o, (0, 2, 1, 3)).reshape(b, s, D_MODEL)
    return o @ w_o


def block_diag_linear(x, w, bias):
    b, s, _ = x.shape
    xb = x.reshape(b, s, RG_BLOCKS, RG_BLOCK_W)
    y = jnp.einsum("bsnc,ncd->bsnd", xb, w).reshape(b, s, D_RNN)
    return y + bias


def rg_lru(x, w_a, b_a, w_x, b_x, lam):
    r = jax.nn.sigmoid(block_diag_linear(x, w_a, b_a).astype(jnp.float32))
    i = jax.nn.sigmoid(block_diag_linear(x, w_x, b_x).astype(jnp.float32))
    log_a = -RG_C * r * jax.nn.softplus(-lam.astype(jnp.float32))
    a = jnp.exp(log_a)
    mult = jnp.sqrt(-jnp.expm1(2.0 * log_a))
    u = mult * i * x.astype(jnp.float32)

    def combine(left, right):
        a_l, b_l = left
        a_r, b_r = right
        return a_l * a_r, a_r * b_l + b_r

    _, h = lax.associative_scan(combine, (a, u), axis=1)
    return h.astype(x.dtype)


def recurrent_mixer(h, w_in, conv_w, conv_b, w_a, b_a, w_x, b_x, lam, w_out):
    proj = h @ w_in
    gate_branch, rec_branch = jnp.split(proj, 2, axis=-1)
    gate = jax.nn.gelu(gate_branch, approximate=True)
    rec = causal_depthwise_conv(rec_branch, conv_w, conv_b)
    rec = rg_lru(rec, w_a, b_a, w_x, b_x, lam)
    return (gate * rec) @ w_out


def conv_ffn(h, w_up, conv_w, conv_b, w_down):
    u = causal_depthwise_conv(h @ w_up, conv_w, conv_b)
    g, val = jnp.split(u, 2, axis=-1)
    return (jax.nn.gelu(g, approximate=True) * val) @ w_down


def setup_inputs(seed: int = 0) -> dict:
    key = jax.random.key(seed)
    ks = iter(jax.random.split(key, 32))

    def nrm(shape, fan_in):
        return jax.random.normal(next(ks), shape, jnp.float32) * (fan_in ** -0.5)

    def gain(shape):
        return 1.0 + 0.02 * jax.random.normal(next(ks), shape, jnp.float32)

    def small(shape):
        return 0.01 * jax.random.normal(next(ks), shape, jnp.float32)

    x = jax.random.normal(next(ks), (BATCH, SEQ, D_MODEL), jnp.float32)

    attn_w_qkv = nrm((N_ATTN_LAYERS, D_MODEL, 3 * D_MODEL), D_MODEL)
    attn_w_o = nrm((N_ATTN_LAYERS, D_MODEL, D_MODEL), D_MODEL)

    rg_w_in = nrm((N_RG_LAYERS, D_MODEL, 2 * D_RNN), D_MODEL)
    rg_conv_w = nrm((N_RG_LAYERS, RG_CONV_W, D_RNN), RG_CONV_W)
    rg_conv_b = small((N_RG_LAYERS, D_RNN))
    rg_w_a = nrm((N_RG_LAYERS, RG_BLOCKS, RG_BLOCK_W, RG_BLOCK_W), RG_BLOCK_W)
    rg_b_a = small((N_RG_LAYERS, D_RNN))
    rg_w_x = nrm((N_RG_LAYERS, RG_BLOCKS, RG_BLOCK_W, RG_BLOCK_W), RG_BLOCK_W)
    rg_b_x = small((N_RG_LAYERS, D_RNN))
    u = jax.random.uniform(next(ks), (N_RG_LAYERS, D_RNN), jnp.float32, 0.9, 0.999)
    a0 = u ** (1.0 / RG_C)
    rg_lambda = jnp.log(a0) - jnp.log1p(-a0)
    rg_w_out = nrm((N_RG_LAYERS, D_RNN, D_MODEL), D_RNN)

    ffn_w_up = nrm((DEPTH, D_MODEL, 2 * D_FF), D_MODEL)
    ffn_conv_w = nrm((DEPTH, FFN_CONV_W, 2 * D_FF), FFN_CONV_W)
    ffn_conv_b = small((DEPTH, 2 * D_FF))
    ffn_w_down = nrm((DEPTH, D_FF, D_MODEL), D_FF)

    mix_pre_g = gain((DEPTH, D_MODEL))
    mix_post_g = gain((DEPTH, D_MODEL))
    ffn_pre_g = gain((DEPTH, D_MODEL))
    ffn_post_g = gain((DEPTH, D_MODEL))

    return {
        "x": x,
        "attn_w_qkv": attn_w_qkv, "attn_w_o": attn_w_o,
        "rg_w_in": rg_w_in, "rg_conv_w": rg_conv_w, "rg_conv_b": rg_conv_b,
        "rg_w_a": rg_w_a, "rg_b_a": rg_b_a, "rg_w_x": rg_w_x, "rg_b_x": rg_b_x,
        "rg_lambda": rg_lambda, "rg_w_out": rg_w_out,
        "ffn_w_up": ffn_w_up, "ffn_conv_w": ffn_conv_w, "ffn_conv_b": ffn_conv_b,
        "ffn_w_down": ffn_w_down,
        "mix_pre_g": mix_pre_g, "mix_post_g": mix_post_g,
        "ffn_pre_g": ffn_pre_g, "ffn_post_g": ffn_post_g,
    }


def reference(x, attn_w_qkv, attn_w_o, rg_w_in, rg_conv_w, rg_conv_b, rg_w_a, rg_b_a,
              rg_w_x, rg_b_x, rg_lambda, rg_w_out, ffn_w_up, ffn_conv_w, ffn_conv_b,
              ffn_w_down, mix_pre_g, mix_post_g, ffn_pre_g, ffn_post_g):
    for i in range(DEPTH):
        j = i // N_MIXERS
        h = rms_norm(x, mix_pre_g[i])
        if i % N_MIXERS == 0:
            h = attention_mixer(h, attn_w_qkv[j], attn_w_o[j])
        else:
            h = recurrent_mixer(h, rg_w_in[j], rg_conv_w[j], rg_conv_b[j], rg_w_a[j], rg_b_a[j],
                                rg_w_x[j], rg_b_x[j], rg_lambda[j], rg_w_out[j])
        x = x + rms_norm(h, mix_post_g[i])
        h = conv_ffn(rms_norm(x, ffn_pre_g[i]), ffn_w_up[i], ffn_conv_w[i], ffn_conv_b[i], ffn_w_down[i])
        x = x + rms_norm(h, ffn_post_g[i])
    return x
```

```python
import functools

import jax
import jax.numpy as jnp
from jax import lax
from jax.experimental import pallas as pl
from jax.experimental.pallas import tpu as pltpu

D_MODEL = 1024
N_HEADS = 16
HEAD_DIM = 64
D_RNN = 1344
RG_BLOCKS = 16
RG_CONV_W = 4
RG_C = 8.0
D_FF = 2816
FFN_CONV_W = 3
NORM_EPS = 1e-6

LANES = 128
SUBLANES = 8
D_RNN_PAD = 1408
VMEM_LIMIT = 56 * 1024 * 1024

ROW_TILE = 512
ATTN_TQ = 256
ATTN_TK = 256
FFN_CHUNK = 256
RG_TILE = 256

F32 = jnp.float32
BF16 = jnp.bfloat16


def _rms(xf, g):
    inv = lax.rsqrt(jnp.mean(xf * xf, axis=-1, keepdims=True) + NORM_EPS)
    return xf * inv * g


def _gelu_tanh(x):
    c = 0.7978845608028654
    return 0.5 * x * (1.0 + jnp.tanh(c * (x + 0.044715 * (x * x * x))))


def _const_spec(shape):
    nd = len(shape)
    return pl.BlockSpec(shape, lambda *_: (0,) * nd, pipeline_mode=pl.Buffered(1))


def _qkv_kernel(x_ref, g_ref, w_ref, q_ref, k_ref, v_ref):
    h = _rms(x_ref[...], g_ref[...]).astype(BF16)
    for c, (o_ref, scale) in enumerate(((q_ref, HEAD_DIM ** -0.5), (k_ref, None), (v_ref, None))):
        y = jnp.dot(h, w_ref[:, c * D_MODEL:(c + 1) * D_MODEL], preferred_element_type=F32)
        if scale is not None:
            y = y * scale
        o_ref[...] = y.astype(BF16)


def _qkv(x, g, w):
    t = x.shape[0]
    row = pl.BlockSpec((ROW_TILE, D_MODEL), lambda i: (i, 0))
    return pl.pallas_call(
        _qkv_kernel,
        grid=(t // ROW_TILE,),
        in_specs=[row, _const_spec((1, D_MODEL)), _const_spec((D_MODEL, 3 * D_MODEL))],
        out_specs=[row, row, row],
        out_shape=[jax.ShapeDtypeStruct((t, D_MODEL), BF16)] * 3,
        compiler_params=pltpu.CompilerParams(
            dimension_semantics=("arbitrary",), vmem_limit_bytes=VMEM_LIMIT),
        name="qkv_proj",
    )(x, g, w)


def _attn_kernel(q_ref, k_ref, v_ref, tri_ref, o_ref, acc_ref, carry_ref):
    i = pl.program_id(2)
    q2 = q_ref[...]
    lane = lax.broadcasted_iota(jnp.int32, (1, LANES), 1)
    first = lane < HEAD_DIM
    zero = jnp.zeros_like(q2)
    qs = (jnp.where(first, q2, zero), jnp.where(first, zero, q2))
    tri = tri_ref[...]

    def step(j, diagonal):
        start = pl.multiple_of(j * ATTN_TK, ATTN_TK)
        kb = k_ref[pl.ds(start, ATTN_TK), :]
        vb = v_ref[pl.ds(start, ATTN_TK), :]
        if diagonal:
            r = lax.broadcasted_iota(jnp.int32, (ATTN_TQ, ATTN_TK), 0)
            c = lax.broadcasted_iota(jnp.int32, (ATTN_TQ, ATTN_TK), 1)
            causal = c < r
        for h in range(2):
            z = lax.dot_general(qs[h], kb, (((1,), (1,)), ((), ())),
                                preferred_element_type=F32)
            l = jnp.log(1.0 + jnp.exp(-jnp.abs(z)))
            log_beta = jnp.minimum(z, 0.0) - l
            log_1mb = log_beta - z
            if diagonal:
                log_1mb = jnp.where(causal, log_1mb, 0.0)
            hi = log_1mb.astype(BF16)
            lo = (log_1mb - hi.astype(F32)).astype(BF16)
            stick = jnp.dot(jnp.concatenate([hi, lo], axis=1), tri,
                            preferred_element_type=F32)
            rowsum = jnp.sum(log_1mb, axis=1, keepdims=True)
            if diagonal:
                w = jnp.where(causal, jnp.exp(log_beta + stick), 0.0)
                acc_ref[h] = jnp.dot(w.astype(BF16), vb, preferred_element_type=F32)
                carry_ref[h] = jnp.broadcast_to(rowsum, (ATTN_TQ, LANES))
            else:
                cprev = carry_ref[h]
                w = jnp.exp(log_beta + stick + jnp.concatenate([cprev, cprev], axis=1))
                acc_ref[h] += jnp.dot(w.astype(BF16), vb, preferred_element_type=F32)
                carry_ref[h] = cprev + rowsum

    step(i, True)

    def body(t, _):
        step(i - 1 - t, False)
        return 0

    lax.fori_loop(0, i, body, 0)
    o_ref[...] = jnp.where(first, acc_ref[0], acc_ref[1]).astype(BF16)


def _attention(q, k, v):
    b, s, _ = q.shape
    jj = lax.broadcasted_iota(jnp.int32, (ATTN_TK, ATTN_TK), 0)
    ss = lax.broadcasted_iota(jnp.int32, (ATTN_TK, ATTN_TK), 1)
    tri = (jj > ss).astype(BF16)
    tri = jnp.concatenate([tri, tri], axis=0)
    qo_spec = pl.BlockSpec((None, ATTN_TQ, LANES), lambda bi, hp, i: (bi, i, hp))
    kv_spec = pl.BlockSpec((None, s, LANES), lambda bi, hp, i: (bi, 0, hp))
    return pl.pallas_call(
        _attn_kernel,
        grid=(b, N_HEADS // 2, s // ATTN_TQ),
        in_specs=[qo_spec, kv_spec, kv_spec, _const_spec((2 * ATTN_TK, ATTN_TK))],
        out_specs=qo_spec,
        out_shape=jax.ShapeDtypeStruct((b, s, D_MODEL), BF16),
        scratch_shapes=[pltpu.VMEM((2, ATTN_TQ, LANES), F32),
                        pltpu.VMEM((2, ATTN_TQ, LANES), F32)],
        compiler_params=pltpu.CompilerParams(
            dimension_semantics=("arbitrary", "arbitrary", "arbitrary"),
            vmem_limit_bytes=VMEM_LIMIT),
        name="stickbreak_attn",
    )(q, k, v, tri)


def _proj_res_kernel(o_ref, w_ref, x_ref, g_ref, out_ref):
    y = jnp.dot(o_ref[...], w_ref[...], preferred_element_type=F32)
    out_ref[...] = x_ref[...] + _rms(y, g_ref[...])


def _proj_res(o, w, x, g):
    t = x.shape[0]
    row = pl.BlockSpec((ROW_TILE, D_MODEL), lambda i: (i, 0))
    return pl.pallas_call(
        _proj_res_kernel,
        grid=(t // ROW_TILE,),
        in_specs=[row, _const_spec((D_MODEL, D_MODEL)), row, _const_spec((1, D_MODEL))],
        out_specs=row,
        out_shape=jax.ShapeDtypeStruct((t, D_MODEL), F32),
        compiler_params=pltpu.CompilerParams(
            dimension_semantics=("arbitrary",), vmem_limit_bytes=VMEM_LIMIT),
        name="attn_out_proj",
    )(o, w, x, g)


def _ffn_kernel(x_ref, gpre_ref, gpost_ref, wu_ref, cp_ref, wd_ref, out_ref,
                h_scr, acc_scr, halo_scr):
    n_chunks = wu_ref.shape[0]
    ts = x_ref.shape[0]

    @pl.when(pl.program_id(1) == 0)
    def _():
        halo_scr[...] = jnp.zeros_like(halo_scr)

    x = x_ref[...]
    h_scr[...] = _rms(x, gpre_ref[...]).astype(BF16)
    acc_scr[...] = jnp.zeros_like(acc_scr)
    row = lax.broadcasted_iota(jnp.int32, (ts, 2 * FFN_CHUNK), 0)

    def chunk(c, _):
        u = jnp.dot(h_scr[...], wu_ref[c], preferred_element_type=F32)
        cp = cp_ref[c]
        halo = halo_scr[c]
        u1 = jnp.where(row == 0, halo[7:8], pltpu.roll(u, 1, 0))
        u2 = jnp.where(row == 0, halo[6:7], jnp.where(row == 1, halo[7:8], pltpu.roll(u, 2, 0)))
        halo_scr[c] = u[ts - SUBLANES:ts]
        cu = cp[2:3] * u + cp[1:2] * u1 + cp[0:1] * u2 + cp[3:4]
        act = (_gelu_tanh(cu[:, :FFN_CHUNK]) * cu[:, FFN_CHUNK:]).astype(BF16)
        acc_scr[...] += jnp.dot(act, wd_ref[c], preferred_element_type=F32)
        return 0

    lax.fori_loop(0, n_chunks, chunk, 0)
    out_ref[...] = x + _rms(acc_scr[...], gpost_ref[...])


def _ffn(x, g_pre, g_post, w_up, conv_w, conv_b, w_down):
    b, s, _ = x.shape
    nc = D_FF // FFN_CHUNK

    def split(a):
        lead = a.shape[:-1]
        a = a.reshape(lead + (2, nc, FFN_CHUNK))
        a = jnp.moveaxis(a, -2, 0)
        return a.reshape((nc,) + lead + (2 * FFN_CHUNK,))

    wu = split(w_up.astype(BF16))
    cp = jnp.concatenate(
        [conv_w, conv_b[None], jnp.zeros((SUBLANES - FFN_CONV_W - 1, 2 * D_FF), F32)], axis=0)
    cp = split(cp)
    wd = w_down.astype(BF16).reshape(nc, FFN_CHUNK, D_MODEL)
    xspec = pl.BlockSpec((None, ROW_TILE, D_MODEL), lambda bi, t: (bi, t, 0))
    return pl.pallas_call(
        _ffn_kernel,
        grid=(b, s // ROW_TILE),
        in_specs=[xspec, _const_spec((1, D_MODEL)), _const_spec((1, D_MODEL)),
                  _const_spec(wu.shape), _const_spec(cp.shape), _const_spec(wd.shape)],
        out_specs=xspec,
        out_shape=jax.ShapeDtypeStruct(x.shape, F32),
        scratch_shapes=[pltpu.VMEM((ROW_TILE, D_MODEL), BF16),
                        pltpu.VMEM((ROW_TILE, D_MODEL), F32),
                        pltpu.VMEM((nc, SUBLANES, 2 * FFN_CHUNK), F32)],
        compiler_params=pltpu.CompilerParams(
            dimension_semantics=("arbitrary", "arbitrary"), vmem_limit_bytes=VMEM_LIMIT),
        name="conv_ffn",
    )(x, g_pre, g_post, wu, cp, wd)


def _rglru_kernel(x_ref, gpre_ref, gpost_ref, win_ref, cp_ref, wa_ref, wx_ref, wout_ref,
                  out_ref, halo_scr, hprev_scr, a_scr, u_scr, hs_scr):
    ts = x_ref.shape[0]
    c_pad = D_RNN_PAD

    @pl.when(pl.program_id(1) == 0)
    def _():
        halo_scr[...] = jnp.zeros_like(halo_scr)
        hprev_scr[...] = jnp.zeros_like(hprev_scr)

    x = x_ref[...]
    h = _rms(x, gpre_ref[...]).astype(BF16)
    proj = jnp.dot(h, win_ref[...], preferred_element_type=F32)
    gate = _gelu_tanh(proj[:, :c_pad])
    rb = proj[:, c_pad:]

    cp = cp_ref[...]
    halo = halo_scr[...]
    row = lax.broadcasted_iota(jnp.int32, (ts, c_pad), 0)
    rec = cp[3:4] * rb + cp[4:5]
    for d in range(1, RG_CONV_W):
        shifted = pltpu.roll(rb, d, 0)
        for r in range(d):
            shifted = jnp.where(row == r, halo[SUBLANES - d + r:SUBLANES - d + r + 1], shifted)
        rec = rec + cp[RG_CONV_W - 1 - d:RG_CONV_W - d] * shifted
    halo_scr[...] = rb[ts - SUBLANES:ts]

    recb = rec.astype(BF16)
    r_gate = jax.nn.sigmoid(jnp.dot(recb, wa_ref[...], preferred_element_type=F32) + cp[5:6])
    i_gate = jax.nn.sigmoid(jnp.dot(recb, wx_ref[...], preferred_element_type=F32) + cp[6:7])
    nlam = -cp[7:8]
    softplus = jnp.maximum(nlam, 0.0) + jnp.log1p(jnp.exp(-jnp.abs(nlam)))
    log_a = (-RG_C) * r_gate * softplus
    a = jnp.exp(log_a)
    mult = jnp.sqrt((1.0 + a * a) * jnp.tanh(-log_a))
    a_scr[...] = a
    u_scr[...] = mult * i_gate * rec

    row8 = lax.broadcasted_iota(jnp.int32, (SUBLANES, c_pad), 0)

    def scan_group(g, hprev):
        start = pl.multiple_of(g * SUBLANES, SUBLANES)
        a8 = a_scr[pl.ds(start, SUBLANES), :]
        u8 = u_scr[pl.ds(start, SUBLANES), :]
        for d in (1, 2, 4):
            keep = row8 >= d
            u8 = jnp.where(keep, a8 * pltpu.roll(u8, d, 0) + u8, u8)
            a8 = jnp.where(keep, a8 * pltpu.roll(a8, d, 0), a8)
        h8 = u8 + a8 * hprev
        hs_scr[pl.ds(start, SUBLANES), :] = h8
        return h8[SUBLANES - 1:SUBLANES]

    hlast = lax.fori_loop(0, ts // SUBLANES, scan_group, hprev_scr[...])
    hprev_scr[...] = hlast

    y = jnp.dot((gate * hs_scr[...]).astype(BF16), wout_ref[...], preferred_element_type=F32)
    out_ref[...] = x + _rms(y, gpost_ref[...])


def _block_diag(w):
    n, bw, _ = w.shape
    dense = jnp.einsum("ncd,nm->ncmd", w, jnp.eye(n, dtype=w.dtype)).reshape(n * bw, n * bw)
    pad = D_RNN_PAD - n * bw
    return jnp.pad(dense, ((0, pad), (0, pad))).astype(BF16)


def _rglru(x, g_pre, g_post, w_in, conv_w, conv_b, w_a, b_a, w_x, b_x, lam, w_out):
    b, s, _ = x.shape
    pad = D_RNN_PAD - D_RNN
    padc = lambda a: jnp.pad(a, ((0, 0), (0, pad)))
    win = jnp.concatenate([padc(w_in[:, :D_RNN]), padc(w_in[:, D_RNN:])], axis=1).astype(BF16)
    cp = padc(jnp.concatenate([conv_w, conv_b[None], b_a[None], b_x[None], lam[None]], axis=0))
    wout = jnp.pad(w_out, ((0, pad), (0, 0))).astype(BF16)
    xspec = pl.BlockSpec((None, RG_TILE, D_MODEL), lambda bi, t: (bi, t, 0))
    return pl.pallas_call(
        _rglru_kernel,
        grid=(b, s // RG_TILE),
        in_specs=[xspec, _const_spec((1, D_MODEL)), _const_spec((1, D_MODEL)),
                  _const_spec(win.shape), _const_spec(cp.shape),
                  _const_spec((D_RNN_PAD, D_RNN_PAD)), _const_spec((D_RNN_PAD, D_RNN_PAD)),
                  _const_spec(wout.shape)],
        out_specs=xspec,
        out_shape=jax.ShapeDtypeStruct(x.shape, F32),
        scratch_shapes=[pltpu.VMEM((SUBLANES, D_RNN_PAD), F32),
                        pltpu.VMEM((1, D_RNN_PAD), F32),
                        pltpu.VMEM((RG_TILE, D_RNN_PAD), F32),
                        pltpu.VMEM((RG_TILE, D_RNN_PAD), F32),
                        pltpu.VMEM((RG_TILE, D_RNN_PAD), F32)],
        compiler_params=pltpu.CompilerParams(
            dimension_semantics=("arbitrary", "arbitrary"), vmem_limit_bytes=VMEM_LIMIT),
        name="rglru_mixer",
    )(x, g_pre, g_post, win, cp, _block_diag(w_a), _block_diag(w_x), wout)


def kernel(x, attn_w_qkv, attn_w_o, rg_w_in, rg_conv_w, rg_conv_b, rg_w_a, rg_b_a, rg_w_x, rg_b_x, rg_lambda, rg_w_out, ffn_w_up, ffn_conv_w, ffn_conv_b, ffn_w_down, mix_pre_g, mix_post_g, ffn_pre_g, ffn_post_g):
    b, s, d = x.shape
    t = b * s
    row_g = lambda g: g.reshape(1, d)

    q, k, v = _qkv(x.reshape(t, d), row_g(mix_pre_g[0]), attn_w_qkv[0].astype(BF16))
    o = _attention(q.reshape(b, s, d), k.reshape(b, s, d), v.reshape(b, s, d))
    x = _proj_res(o.reshape(t, d), attn_w_o[0].astype(BF16), x.reshape(t, d), row_g(mix_post_g[0]))
    x = x.reshape(b, s, d)
    x = _ffn(x, row_g(ffn_pre_g[0]), row_g(ffn_post_g[0]),
             ffn_w_up[0], ffn_conv_w[0], ffn_conv_b[0], ffn_w_down[0])

    x = _rglru(x, row_g(mix_pre_g[1]), row_g(mix_post_g[1]), rg_w_in[0], rg_conv_w[0],
               rg_conv_b[0], rg_w_a[0], rg_b_a[0], rg_w_x[0], rg_b_x[0], rg_lambda[0], rg_w_out[0])
    x = _ffn(x, row_g(ffn_pre_g[1]), row_g(ffn_post_g[1]),
             ffn_w_up[1], ffn_conv_w[1], ffn_conv_b[1], ffn_w_down[1])
    return x
```

```python
import functools

import jax
import jax.numpy as jnp
from jax import lax
from jax.experimental import pallas as pl
from jax.experimental.pallas import tpu as pltpu

D_MODEL = 1024
N_HEADS = 16
HEAD_DIM = 64
D_RNN = 1344
RG_BLOCKS = 16
RG_CONV_W = 4
RG_C = 8.0
D_FF = 2816
FFN_CONV_W = 3
NORM_EPS = 1e-6

LANES = 128
SUBLANES = 8
D_RNN_PAD = 1408
VMEM_LIMIT = 56 * 1024 * 1024

ROW_TILE = 512
ATTN_TQ = 256
ATTN_TK = 256
FFN_CHUNK = 256
RG_TILE = 256
EXIT_LOG = -110.0

F32 = jnp.float32
BF16 = jnp.bfloat16


def _rms(xf, g):
    inv = lax.rsqrt(jnp.mean(xf * xf, axis=-1, keepdims=True) + NORM_EPS)
    return xf * inv * g


def _gelu_tanh(x):
    c = 0.7978845608028654
    return 0.5 * x * (1.0 + jnp.tanh(c * (x + 0.044715 * (x * x * x))))


def _const_spec(shape):
    nd = len(shape)
    return pl.BlockSpec(shape, lambda *_: (0,) * nd, pipeline_mode=pl.Buffered(1))


def _qkv_kernel(x_ref, g_ref, w_ref, q_ref, k_ref, v_ref):
    h = _rms(x_ref[...], g_ref[...]).astype(BF16)
    for c, (o_ref, scale) in enumerate(((q_ref, HEAD_DIM ** -0.5), (k_ref, None), (v_ref, None))):
        y = jnp.dot(h, w_ref[:, c * D_MODEL:(c + 1) * D_MODEL], preferred_element_type=F32)
        if scale is not None:
            y = y * scale
        o_ref[...] = y.astype(BF16)


def _qkv(x, g, w):
    t = x.shape[0]
    row = pl.BlockSpec((ROW_TILE, D_MODEL), lambda i: (i, 0))
    return pl.pallas_call(
        _qkv_kernel,
        grid=(t // ROW_TILE,),
        in_specs=[row, _const_spec((1, D_MODEL)), _const_spec((D_MODEL, 3 * D_MODEL))],
        out_specs=[row, row, row],
        out_shape=[jax.ShapeDtypeStruct((t, D_MODEL), BF16)] * 3,
        compiler_params=pltpu.CompilerParams(
            dimension_semantics=("arbitrary",), vmem_limit_bytes=VMEM_LIMIT),
        name="qkv_proj",
    )(x, g, w)


def _attn_kernel(q_ref, k_ref, v_ref, tri_ref, o_ref, acc_ref, carry_ref):
    i = pl.program_id(2)
    q2 = q_ref[...]
    lane = lax.broadcasted_iota(jnp.int32, (1, LANES), 1)
    first = lane < HEAD_DIM
    zero = jnp.zeros_like(q2)
    qs = (jnp.where(first, q2, zero), jnp.where(first, zero, q2))
    tri = tri_ref[...]

    def step(j, diagonal):
        start = pl.multiple_of(j * ATTN_TK, ATTN_TK)
        kb = k_ref[pl.ds(start, ATTN_TK), :]
        vb = v_ref[pl.ds(start, ATTN_TK), :]
        if diagonal:
            r = lax.broadcasted_iota(jnp.int32, (ATTN_TQ, ATTN_TK), 0)
            c = lax.broadcasted_iota(jnp.int32, (ATTN_TQ, ATTN_TK), 1)
            causal = c < r
        carry_max = None
        for h in range(2):
            z = lax.dot_general(qs[h], kb, (((1,), (1,)), ((), ())),
                                preferred_element_type=F32)
            l = jnp.log(1.0 + jnp.exp(-jnp.abs(z)))
            log_beta = jnp.minimum(z, 0.0) - l
            log_1mb = log_beta - z
            if diagonal:
                log_1mb = jnp.where(causal, log_1mb, 0.0)
            stick = jnp.dot(log_1mb.astype(BF16), tri, preferred_element_type=F32)
            rowsum = jnp.sum(log_1mb, axis=1, keepdims=True)
            if diagonal:
                w = jnp.where(causal, jnp.exp(log_beta + stick), 0.0)
                acc_ref[h] = jnp.dot(w.astype(BF16), vb, preferred_element_type=F32)
                carry = jnp.broadcast_to(rowsum, (ATTN_TQ, LANES))
            else:
                cprev = carry_ref[h]
                w = jnp.exp(log_beta + stick + jnp.concatenate([cprev, cprev], axis=1))
                acc_ref[h] += jnp.dot(w.astype(BF16), vb, preferred_element_type=F32)
                carry = cprev + rowsum
            carry_ref[h] = carry
            m = jnp.max(carry)
            carry_max = m if carry_max is None else jnp.maximum(carry_max, m)
        return carry_max

    def cond(state):
        j, carry_max = state
        return jnp.logical_and(j >= 0, carry_max > EXIT_LOG)

    def body(state):
        j, _ = state
        return j - 1, step(j, False)

    lax.while_loop(cond, body, (i - 1, step(i, True)))
    o_ref[...] = jnp.where(first, acc_ref[0], acc_ref[1]).astype(BF16)


def _attention(q, k, v):
    b, s, _ = q.shape
    jj = lax.broadcasted_iota(jnp.int32, (ATTN_TK, ATTN_TK), 0)
    ss = lax.broadcasted_iota(jnp.int32, (ATTN_TK, ATTN_TK), 1)
    tri = (jj > ss).astype(BF16)
    qo_spec = pl.BlockSpec((None, ATTN_TQ, LANES), lambda bi, hp, i: (bi, i, hp))
    kv_spec = pl.BlockSpec((None, s, LANES), lambda bi, hp, i: (bi, 0, hp))
    return pl.pallas_call(
        _attn_kernel,
        grid=(b, N_HEADS // 2, s // ATTN_TQ),
        in_specs=[qo_spec, kv_spec, kv_spec, _const_spec((ATTN_TK, ATTN_TK))],
        out_specs=qo_spec,
        out_shape=jax.ShapeDtypeStruct((b, s, D_MODEL), BF16),
        scratch_shapes=[pltpu.VMEM((2, ATTN_TQ, LANES), F32),
                        pltpu.VMEM((2, ATTN_TQ, LANES), F32)],
        compiler_params=pltpu.CompilerParams(
            dimension_semantics=("arbitrary", "arbitrary", "arbitrary"),
            vmem_limit_bytes=VMEM_LIMIT),
        name="stickbreak_attn",
    )(q, k, v, tri)


def _proj_res_kernel(o_ref, w_ref, x_ref, g_ref, out_ref):
    y = jnp.dot(o_ref[...], w_ref[...], preferred_element_type=F32)
    out_ref[...] = x_ref[...] + _rms(y, g_ref[...])


def _proj_res(o, w, x, g):
    t = x.shape[0]
    row = pl.BlockSpec((ROW_TILE, D_MODEL), lambda i: (i, 0))
    return pl.pallas_call(
        _proj_res_kernel,
        grid=(t // ROW_TILE,),
        in_specs=[row, _const_spec((D_MODEL, D_MODEL)), row, _const_spec((1, D_MODEL))],
        out_specs=row,
        out_shape=jax.ShapeDtypeStruct((t, D_MODEL), F32),
        compiler_params=pltpu.CompilerParams(
            dimension_semantics=("arbitrary",), vmem_limit_bytes=VMEM_LIMIT),
        name="attn_out_proj",
    )(o, w, x, g)


def _ffn_kernel(x_ref, gpre_ref, gpost_ref, wu_ref, cp_ref, wd_ref, out_ref,
                h_scr, acc_scr, halo_scr, u_scr):
    n_chunks = wu_ref.shape[0]
    ts = x_ref.shape[0]

    @pl.when(pl.program_id(1) == 0)
    def _():
        halo_scr[...] = jnp.zeros_like(halo_scr)

    x = x_ref[...]
    h_scr[...] = _rms(x, gpre_ref[...]).astype(BF16)
    acc_scr[...] = jnp.zeros_like(acc_scr)

    def chunk(c, slot):
        u = jnp.dot(h_scr[...], wu_ref[c], preferred_element_type=F32)
        us = u_scr.at[slot]
        us[0:SUBLANES] = halo_scr[c]
        us[SUBLANES:SUBLANES + ts] = u
        halo_scr[c] = u[ts - SUBLANES:ts]
        cp = cp_ref[c]
        cu = (cp[2:3] * u + cp[1:2] * us[SUBLANES - 1:SUBLANES - 1 + ts]
              + cp[0:1] * us[SUBLANES - 2:SUBLANES - 2 + ts] + cp[3:4])
        act = (_gelu_tanh(cu[:, :FFN_CHUNK]) * cu[:, FFN_CHUNK:]).astype(BF16)
        acc_scr[...] += jnp.dot(act, wd_ref[c], preferred_element_type=F32)

    lead = n_chunks % 2
    if lead:
        chunk(0, 0)

    def pair(p, _):
        chunk(lead + 2 * p, 0)
        chunk(lead + 2 * p + 1, 1)
        return 0

    lax.fori_loop(0, n_chunks // 2, pair, 0)
    out_ref[...] = x + _rms(acc_scr[...], gpost_ref[...])


def _ffn(x, g_pre, g_post, w_up, conv_w, conv_b, w_down):
    b, s, _ = x.shape
    nc = D_FF // FFN_CHUNK

    def split(a):
        lead = a.shape[:-1]
        a = a.reshape(lead + (2, nc, FFN_CHUNK))
        a = jnp.moveaxis(a, -2, 0)
        return a.reshape((nc,) + lead + (2 * FFN_CHUNK,))

    wu = split(w_up.astype(BF16))
    cp = jnp.concatenate(
        [conv_w, conv_b[None], jnp.zeros((SUBLANES - FFN_CONV_W - 1, 2 * D_FF), F32)], axis=0)
    cp = split(cp)
    wd = w_down.astype(BF16).reshape(nc, FFN_CHUNK, D_MODEL)
    xspec = pl.BlockSpec((None, ROW_TILE, D_MODEL), lambda bi, t: (bi, t, 0))
    return pl.pallas_call(
        _ffn_kernel,
        grid=(b, s // ROW_TILE),
        in_specs=[xspec, _const_spec((1, D_MODEL)), _const_spec((1, D_MODEL)),
                  _const_spec(wu.shape), _const_spec(cp.shape), _const_spec(wd.shape)],
        out_specs=xspec,
        out_shape=jax.ShapeDtypeStruct(x.shape, F32),
        scratch_shapes=[pltpu.VMEM((ROW_TILE, D_MODEL), BF16),
                        pltpu.VMEM((ROW_TILE, D_MODEL), F32),
                        pltpu.VMEM((nc, SUBLANES, 2 * FFN_CHUNK), F32),
                        pltpu.VMEM((2, SUBLANES + ROW_TILE, 2 * FFN_CHUNK), F32)],
        compiler_params=pltpu.CompilerParams(
            dimension_semantics=("arbitrary", "arbitrary"), vmem_limit_bytes=VMEM_LIMIT),
        name="conv_ffn",
    )(x, g_pre, g_post, wu, cp, wd)


def _gate_windows():
    bw = D_RNN // RG_BLOCKS
    out = []
    for col0 in range(0, D_RNN_PAD, 2 * LANES):
        ncols = min(2 * LANES, D_RNN_PAD - col0)
        b_lo = col0 // bw
        b_hi = min((col0 + ncols - 1) // bw, RG_BLOCKS - 1)
        row0 = (bw * b_lo) // LANES * LANES
        row1 = min(-(-(bw * (b_hi + 1)) // LANES) * LANES, D_RNN_PAD)
        out.append((col0, ncols, row0, row1 - row0))
    return out


GATE_WINDOWS = _gate_windows()
GATE_K = max(w[3] for w in GATE_WINDOWS)


def _rglru_kernel(x_ref, gpre_ref, gpost_ref, win_ref, cp_ref, wg_ref, wout_ref,
                  out_ref, rb_scr, hprev_scr, a_scr, u_scr, hs_scr):
    ts = x_ref.shape[0]
    c_pad = D_RNN_PAD

    @pl.when(pl.program_id(1) == 0)
    def _():
        rb_scr[0:SUBLANES] = jnp.zeros((SUBLANES, c_pad), F32)
        hprev_scr[...] = jnp.zeros_like(hprev_scr)

    x = x_ref[...]
    h = _rms(x, gpre_ref[...]).astype(BF16)
    proj = jnp.dot(h, win_ref[...], preferred_element_type=F32)
    gate = _gelu_tanh(proj[:, :c_pad])
    rb = proj[:, c_pad:]

    cp = cp_ref[...]
    rb_scr[SUBLANES:SUBLANES + ts] = rb
    rec = cp[RG_CONV_W - 1:RG_CONV_W] * rb + cp[RG_CONV_W:RG_CONV_W + 1]
    for d in range(1, RG_CONV_W):
        rec = rec + cp[RG_CONV_W - 1 - d:RG_CONV_W - d] * rb_scr[SUBLANES - d:SUBLANES - d + ts]
    rb_scr[0:SUBLANES] = rb[ts - SUBLANES:ts]

    recb = rec.astype(BF16)
    ra, rx = [], []
    for n, (_, ncols, row0, nrows) in enumerate(GATE_WINDOWS):
        res = jnp.dot(recb[:, row0:row0 + nrows], wg_ref[n, 0:nrows, :],
                      preferred_element_type=F32)
        ra.append(res[:, 0:ncols])
        rx.append(res[:, 2 * LANES:2 * LANES + ncols])
    r_gate = jax.nn.sigmoid(jnp.concatenate(ra, axis=1) + cp[5:6])
    i_gate = jax.nn.sigmoid(jnp.concatenate(rx, axis=1) + cp[6:7])
    nlam = -cp[7:8]
    softplus = jnp.maximum(nlam, 0.0) + jnp.log1p(jnp.exp(-jnp.abs(nlam)))
    log_a = (-RG_C) * r_gate * softplus
    a = jnp.exp(log_a)
    mult = jnp.sqrt((1.0 + a * a) * jnp.tanh(-log_a))
    a_scr[...] = a
    u_scr[...] = mult * i_gate * rec

    row8 = lax.broadcasted_iota(jnp.int32, (SUBLANES, c_pad), 0)

    def scan_group(g, hprev):
        start = pl.multiple_of(g * SUBLANES, SUBLANES)
        a8 = a_scr[pl.ds(start, SUBLANES), :]
        u8 = u_scr[pl.ds(start, SUBLANES), :]
        for d in (1, 2, 4):
            keep = row8 >= d
            u8 = jnp.where(keep, a8 * pltpu.roll(u8, d, 0) + u8, u8)
            a8 = jnp.where(keep, a8 * pltpu.roll(a8, d, 0), a8)
        h8 = u8 + a8 * hprev
        hs_scr[pl.ds(start, SUBLANES), :] = h8
        return h8[SUBLANES - 1:SUBLANES]

    hlast = lax.fori_loop(0, ts // SUBLANES, scan_group, hprev_scr[...])
    hprev_scr[...] = hlast

    y = jnp.dot((gate * hs_scr[...]).astype(BF16), wout_ref[...], preferred_element_type=F32)
    out_ref[...] = x + _rms(y, gpost_ref[...])


def _gate_slabs(w_a, w_x):
    def dense(w):
        n, bw, _ = w.shape
        d = jnp.einsum("ncd,nm->ncmd", w, jnp.eye(n, dtype=w.dtype)).reshape(n * bw, n * bw)
        pad = D_RNN_PAD - n * bw
        return jnp.pad(d, ((0, pad), (0, pad)))

    da, dx = dense(w_a), dense(w_x)
    slabs = []
    for col0, ncols, row0, nrows in GATE_WINDOWS:
        pad = ((0, GATE_K - nrows), (0, 2 * LANES - ncols))
        slabs.append(jnp.concatenate(
            [jnp.pad(d[row0:row0 + nrows, col0:col0 + ncols], pad) for d in (da, dx)], axis=1))
    return jnp.stack(slabs).astype(BF16)


def _rglru(x, g_pre, g_post, w_in, conv_w, conv_b, w_a, b_a, w_x, b_x, lam, w_out):
    b, s, _ = x.shape
    pad = D_RNN_PAD - D_RNN
    padc = lambda a: jnp.pad(a, ((0, 0), (0, pad)))
    win = jnp.concatenate([padc(w_in[:, :D_RNN]), padc(w_in[:, D_RNN:])], axis=1).astype(BF16)
    cp = padc(jnp.concatenate([conv_w, conv_b[None], b_a[None], b_x[None], lam[None]], axis=0))
    wout = jnp.pad(w_out, ((0, pad), (0, 0))).astype(BF16)
    wg = _gate_slabs(w_a, w_x)
    xspec = pl.BlockSpec((None, RG_TILE, D_MODEL), lambda bi, t: (bi, t, 0))
    return pl.pallas_call(
        _rglru_kernel,
        grid=(b, s // RG_TILE),
        in_specs=[xspec, _const_spec((1, D_MODEL)), _const_spec((1, D_MODEL)),
                  _const_spec(win.shape), _const_spec(cp.shape), _const_spec(wg.shape),
                  _const_spec(wout.shape)],
        out_specs=xspec,
        out_shape=jax.ShapeDtypeStruct(x.shape, F32),
        scratch_shapes=[pltpu.VMEM((SUBLANES + RG_TILE, D_RNN_PAD), F32),
                        pltpu.VMEM((1, D_RNN_PAD), F32),
                        pltpu.VMEM((RG_TILE, D_RNN_PAD), F32),
                        pltpu.VMEM((RG_TILE, D_RNN_PAD), F32),
                        pltpu.VMEM((RG_TILE, D_RNN_PAD), F32)],
        compiler_params=pltpu.CompilerParams(
            dimension_semantics=("arbitrary", "arbitrary"), vmem_limit_bytes=VMEM_LIMIT),
        name="rglru_mixer",
    )(x, g_pre, g_post, win, cp, wg, wout)


def kernel(x, attn_w_qkv, attn_w_o, rg_w_in, rg_conv_w, rg_conv_b, rg_w_a, rg_b_a, rg_w_x, rg_b_x, rg_lambda, rg_w_out, ffn_w_up, ffn_conv_w, ffn_conv_b, ffn_w_down, mix_pre_g, mix_post_g, ffn_pre_g, ffn_post_g):
    b, s, d = x.shape
    t = b * s
    row_g = lambda g: g.reshape(1, d)

    q, k, v = _qkv(x.reshape(t, d), row_g(mix_pre_g[0]), attn_w_qkv[0].astype(BF16))
    o = _attention(q.reshape(b, s, d), k.reshape(b, s, d), v.reshape(b, s, d))
    x = _proj_res(o.reshape(t, d), attn_w_o[0].astype(BF16), x.reshape(t, d), row_g(mix_post_g[0]))
    x = x.reshape(b, s, d)
    x = _ffn(x, row_g(ffn_pre_g[0]), row_g(ffn_post_g[0]),
             ffn_w_up[0], ffn_conv_w[0], ffn_conv_b[0], ffn_w_down[0])

    x = _rglru(x, row_g(mix_pre_g[1]), row_g(mix_post_g[1]), rg_w_in[0], rg_conv_w[0],
               rg_conv_b[0], rg_w_a[0], rg_b_a[0], rg_w_x[0], rg_b_x[0], rg_lambda[0], rg_w_out[0])
    x = _ffn(x, row_g(ffn_pre_g[1]), row_g(ffn_post_g[1]),
             ffn_w_up[1], ffn_conv_w[1], ffn_conv_b[1], ffn_w_down[1])
    return x
```

```python
import functools

import jax
import jax.numpy as jnp
from jax import lax
from jax.experimental import pallas as pl
from jax.experimental.pallas import tpu as pltpu

D_MODEL = 1024
N_HEADS = 16
HEAD_DIM = 64
D_RNN = 1344
RG_BLOCKS = 16
RG_CONV_W = 4
RG_C = 8.0
D_FF = 2816
FFN_CONV_W = 3
NORM_EPS = 1e-6

LANES = 128
SUBLANES = 8
D_RNN_PAD = 1408
VMEM_LIMIT = 56 * 1024 * 1024

ROW_TILE = 512
ATTN_TQ = 256
ATTN_TK = 256
ATTN_PAIRS = 4
FFN_CHUNK = 256
RG_TILE = 256
EXIT_LOG = -110.0

F32 = jnp.float32
BF16 = jnp.bfloat16


def _rms(xf, g):
    inv = lax.rsqrt(jnp.mean(xf * xf, axis=-1, keepdims=True) + NORM_EPS)
    return xf * inv * g


GELU_C0 = 0.7978845608028654
GELU_C1 = GELU_C0 * 0.044715


def _gelu_tanh(x):
    return 0.5 * x * (1.0 + jnp.tanh(GELU_C0 * (x + 0.044715 * (x * x * x))))


def _const_spec(shape):
    nd = len(shape)
    return pl.BlockSpec(shape, lambda *_: (0,) * nd, pipeline_mode=pl.Buffered(1))


def _qkv_kernel(x_ref, g_ref, w_ref, q_ref, k_ref, v_ref):
    h = _rms(x_ref[...], g_ref[...]).astype(BF16)
    for c, (o_ref, scale) in enumerate(((q_ref, HEAD_DIM ** -0.5), (k_ref, None), (v_ref, None))):
        y = jnp.dot(h, w_ref[:, c * D_MODEL:(c + 1) * D_MODEL], preferred_element_type=F32)
        if scale is not None:
            y = y * scale
        o_ref[...] = y.astype(BF16)


def _qkv(x, g, w):
    t = x.shape[0]
    row = pl.BlockSpec((ROW_TILE, D_MODEL), lambda i: (i, 0))
    return pl.pallas_call(
        _qkv_kernel,
        grid=(t // ROW_TILE,),
        in_specs=[row, _const_spec((1, D_MODEL)), _const_spec((D_MODEL, 3 * D_MODEL))],
        out_specs=[row, row, row],
        out_shape=[jax.ShapeDtypeStruct((t, D_MODEL), BF16)] * 3,
        compiler_params=pltpu.CompilerParams(
            dimension_semantics=("arbitrary",), vmem_limit_bytes=VMEM_LIMIT),
        name="qkv_proj",
    )(x, g, w)


def _attn_kernel(q_ref, k_ref, v_ref, tri_ref, o_ref, acc_ref, carry_ref):
    i = pl.program_id(2)
    lane = lax.broadcasted_iota(jnp.int32, (1, LANES), 1)
    first = lane < HEAD_DIM
    qs = []
    for p in range(ATTN_PAIRS):
        q2 = q_ref[:, p * LANES:(p + 1) * LANES]
        zero = jnp.zeros_like(q2)
        qs += [jnp.where(first, q2, zero), jnp.where(first, zero, q2)]
    tri = tri_ref[...]

    def step(j, diagonal):
        start = pl.multiple_of(j * ATTN_TK, ATTN_TK)
        if diagonal:
            r = lax.broadcasted_iota(jnp.int32, (ATTN_TQ, ATTN_TK), 0)
            c = lax.broadcasted_iota(jnp.int32, (ATTN_TQ, ATTN_TK), 1)
            causal = c < r
        carry_max = None
        for h in range(2 * ATTN_PAIRS):
            cols = slice((h // 2) * LANES, (h // 2 + 1) * LANES)
            kb = k_ref[pl.ds(start, ATTN_TK), cols]
            vb = v_ref[pl.ds(start, ATTN_TK), cols]
            z = lax.dot_general(qs[h], kb, (((1,), (1,)), ((), ())),
                                preferred_element_type=F32)
            l = jnp.log(1.0 + jnp.exp(-jnp.abs(z)))
            log_beta = jnp.minimum(z, 0.0) - l
            log_1mb = log_beta - z
            if diagonal:
                log_1mb = jnp.where(causal, log_1mb, 0.0)
            stick = jnp.dot(log_1mb.astype(BF16), tri, preferred_element_type=F32)
            rowsum = jnp.sum(log_1mb, axis=1, keepdims=True)
            if diagonal:
                w = jnp.where(causal, jnp.exp(log_beta + stick), 0.0)
                acc_ref[h] = jnp.dot(w.astype(BF16), vb, preferred_element_type=F32)
                carry = jnp.broadcast_to(rowsum, (ATTN_TQ, LANES))
            else:
                cprev = carry_ref[h]
                w = jnp.exp(log_beta + stick + jnp.concatenate([cprev, cprev], axis=1))
                acc_ref[h] += jnp.dot(w.astype(BF16), vb, preferred_element_type=F32)
                carry = cprev + rowsum
            carry_ref[h] = carry
            m = jnp.max(carry)
            carry_max = m if carry_max is None else jnp.maximum(carry_max, m)
        return carry_max

    def cond(state):
        j, carry_max = state
        return jnp.logical_and(j >= 0, carry_max > EXIT_LOG)

    def body(state):
        j, _ = state
        return j - 1, step(j, False)

    lax.while_loop(cond, body, (i - 1, step(i, True)))
    for p in range(ATTN_PAIRS):
        o_ref[:, p * LANES:(p + 1) * LANES] = jnp.where(
            first, acc_ref[2 * p], acc_ref[2 * p + 1]).astype(BF16)


def _attention(q, k, v):
    b, s, _ = q.shape
    jj = lax.broadcasted_iota(jnp.int32, (ATTN_TK, ATTN_TK), 0)
    ss = lax.broadcasted_iota(jnp.int32, (ATTN_TK, ATTN_TK), 1)
    tri = (jj > ss).astype(BF16)
    width = ATTN_PAIRS * LANES
    qo_spec = pl.BlockSpec((None, ATTN_TQ, width), lambda bi, hp, i: (bi, i, hp))
    kv_spec = pl.BlockSpec((None, s, width), lambda bi, hp, i: (bi, 0, hp))
    return pl.pallas_call(
        _attn_kernel,
        grid=(b, N_HEADS // (2 * ATTN_PAIRS), s // ATTN_TQ),
        in_specs=[qo_spec, kv_spec, kv_spec, _const_spec((ATTN_TK, ATTN_TK))],
        out_specs=qo_spec,
        out_shape=jax.ShapeDtypeStruct((b, s, D_MODEL), BF16),
        scratch_shapes=[pltpu.VMEM((2 * ATTN_PAIRS, ATTN_TQ, LANES), F32),
                        pltpu.VMEM((2 * ATTN_PAIRS, ATTN_TQ, LANES), F32)],
        compiler_params=pltpu.CompilerParams(
            dimension_semantics=("arbitrary", "arbitrary", "arbitrary"),
            vmem_limit_bytes=VMEM_LIMIT),
        name="stickbreak_attn",
    )(q, k, v, tri)


def _proj_res_kernel(o_ref, w_ref, x_ref, g_ref, out_ref):
    y = jnp.dot(o_ref[...], w_ref[...], preferred_element_type=F32)
    out_ref[...] = x_ref[...] + _rms(y, g_ref[...])


def _proj_res(o, w, x, g):
    t = x.shape[0]
    row = pl.BlockSpec((ROW_TILE, D_MODEL), lambda i: (i, 0))
    return pl.pallas_call(
        _proj_res_kernel,
        grid=(t // ROW_TILE,),
        in_specs=[row, _const_spec((D_MODEL, D_MODEL)), row, _const_spec((1, D_MODEL))],
        out_specs=row,
        out_shape=jax.ShapeDtypeStruct((t, D_MODEL), F32),
        compiler_params=pltpu.CompilerParams(
            dimension_semantics=("arbitrary",), vmem_limit_bytes=VMEM_LIMIT),
        name="attn_out_proj",
    )(o, w, x, g)


def _ffn_kernel(x_ref, gpre_ref, gpost_ref, wu_ref, cp_ref, wd_ref, out_ref,
                h_scr, acc_scr, halo_scr, u_scr):
    n_chunks = wu_ref.shape[0]
    ts = x_ref.shape[0]

    @pl.when(pl.program_id(1) == 0)
    def _():
        halo_scr[...] = jnp.zeros_like(halo_scr)

    x = x_ref[...]
    h_scr[...] = _rms(x, gpre_ref[...]).astype(BF16)
    acc_scr[...] = jnp.zeros_like(acc_scr)

    def up(c, slot):
        u = jnp.dot(h_scr[...], wu_ref[c], preferred_element_type=F32)
        us = u_scr.at[slot]
        us[0:SUBLANES] = halo_scr[c]
        us[SUBLANES:SUBLANES + ts] = u
        halo_scr[c] = u[ts - SUBLANES:ts]

    def down(c, slot):
        us = u_scr.at[slot]
        cp = cp_ref[c]
        cu = (cp[2:3] * us[SUBLANES:SUBLANES + ts] + cp[1:2] * us[SUBLANES - 1:SUBLANES - 1 + ts]
              + cp[0:1] * us[SUBLANES - 2:SUBLANES - 2 + ts] + cp[3:4])
        g = cu[:, :FFN_CHUNK]
        half_val = cu[:, FFN_CHUNK:]
        th = jnp.tanh(g * (GELU_C0 + GELU_C1 * (g * g)))
        act = ((g * th + g) * half_val).astype(BF16)
        acc_scr[...] += jnp.dot(act, wd_ref[c], preferred_element_type=F32)

    assert n_chunks % 2 == 1
    up(0, 0)

    def pair(p, _):
        c = 2 * p
        up(c + 1, 1)
        down(c, 0)
        up(c + 2, 0)
        down(c + 1, 1)
        return 0

    lax.fori_loop(0, n_chunks // 2, pair, 0)
    down(n_chunks - 1, 0)
    out_ref[...] = x + _rms(acc_scr[...], gpost_ref[...])


def _ffn(x, g_pre, g_post, w_up, conv_w, conv_b, w_down):
    b, s, _ = x.shape
    nc = D_FF // FFN_CHUNK

    def split(a):
        lead = a.shape[:-1]
        a = a.reshape(lead + (2, nc, FFN_CHUNK))
        a = jnp.moveaxis(a, -2, 0)
        return a.reshape((nc,) + lead + (2 * FFN_CHUNK,))

    wu = split(w_up.astype(BF16))
    cp = jnp.concatenate(
        [conv_w, conv_b[None], jnp.zeros((SUBLANES - FFN_CONV_W - 1, 2 * D_FF), F32)], axis=0)
    cp = cp * jnp.where(jnp.arange(2 * D_FF) < D_FF, 1.0, 0.5).astype(F32)
    cp = split(cp)
    wd = w_down.astype(BF16).reshape(nc, FFN_CHUNK, D_MODEL)
    xspec = pl.BlockSpec((None, ROW_TILE, D_MODEL), lambda bi, t: (bi, t, 0))
    return pl.pallas_call(
        _ffn_kernel,
        grid=(b, s // ROW_TILE),
        in_specs=[xspec, _const_spec((1, D_MODEL)), _const_spec((1, D_MODEL)),
                  _const_spec(wu.shape), _const_spec(cp.shape), _const_spec(wd.shape)],
        out_specs=xspec,
        out_shape=jax.ShapeDtypeStruct(x.shape, F32),
        scratch_shapes=[pltpu.VMEM((ROW_TILE, D_MODEL), BF16),
                        pltpu.VMEM((ROW_TILE, D_MODEL), F32),
                        pltpu.VMEM((nc, SUBLANES, 2 * FFN_CHUNK), F32),
                        pltpu.VMEM((2, SUBLANES + ROW_TILE, 2 * FFN_CHUNK), F32)],
        compiler_params=pltpu.CompilerParams(
            dimension_semantics=("arbitrary", "arbitrary"), vmem_limit_bytes=VMEM_LIMIT),
        name="conv_ffn",
    )(x, g_pre, g_post, wu, cp, wd)


def _gate_windows():
    bw = D_RNN // RG_BLOCKS
    out = []
    for col0 in range(0, D_RNN_PAD, 2 * LANES):
        ncols = min(2 * LANES, D_RNN_PAD - col0)
        b_lo = col0 // bw
        b_hi = min((col0 + ncols - 1) // bw, RG_BLOCKS - 1)
        row0 = (bw * b_lo) // LANES * LANES
        row1 = min(-(-(bw * (b_hi + 1)) // LANES) * LANES, D_RNN_PAD)
        out.append((col0, ncols, row0, row1 - row0))
    return out


GATE_WINDOWS = _gate_windows()
GATE_K = max(w[3] for w in GATE_WINDOWS)


def _rglru_kernel(x_ref, gpre_ref, gpost_ref, win_ref, cp_ref, wg_ref, wout_ref,
                  out_ref, rb_scr, hprev_scr, a_scr, u_scr, hs_scr):
    ts = x_ref.shape[0]
    c_pad = D_RNN_PAD

    @pl.when(pl.program_id(1) == 0)
    def _():
        rb_scr[0:SUBLANES] = jnp.zeros((SUBLANES, c_pad), F32)
        hprev_scr[...] = jnp.zeros_like(hprev_scr)

    x = x_ref[...]
    h = _rms(x, gpre_ref[...]).astype(BF16)
    proj = jnp.dot(h, win_ref[...], preferred_element_type=F32)
    gate = _gelu_tanh(proj[:, :c_pad])
    rb = proj[:, c_pad:]

    cp = cp_ref[...]
    rb_scr[SUBLANES:SUBLANES + ts] = rb
    rec = cp[RG_CONV_W - 1:RG_CONV_W] * rb + cp[RG_CONV_W:RG_CONV_W + 1]
    for d in range(1, RG_CONV_W):
        rec = rec + cp[RG_CONV_W - 1 - d:RG_CONV_W - d] * rb_scr[SUBLANES - d:SUBLANES - d + ts]
    rb_scr[0:SUBLANES] = rb[ts - SUBLANES:ts]

    recb = rec.astype(BF16)
    ra, rx = [], []
    for n, (_, ncols, row0, nrows) in enumerate(GATE_WINDOWS):
        res = jnp.dot(recb[:, row0:row0 + nrows], wg_ref[n, 0:nrows, :],
                      preferred_element_type=F32)
        ra.append(res[:, 0:ncols])
        rx.append(res[:, 2 * LANES:2 * LANES + ncols])
    r_gate = jax.nn.sigmoid(jnp.concatenate(ra, axis=1) + cp[5:6])
    i_gate = jax.nn.sigmoid(jnp.concatenate(rx, axis=1) + cp[6:7])
    nlam = -cp[7:8]
    softplus = jnp.maximum(nlam, 0.0) + jnp.log1p(jnp.exp(-jnp.abs(nlam)))
    log_a = (-RG_C) * r_gate * softplus
    a = jnp.exp(log_a)
    mult = jnp.sqrt((1.0 + a * a) * jnp.tanh(-log_a))
    a_scr[...] = a
    u_scr[...] = mult * i_gate * rec

    row8 = lax.broadcasted_iota(jnp.int32, (SUBLANES, c_pad), 0)

    def scan_group(g, hprev):
        start = pl.multiple_of(g * SUBLANES, SUBLANES)
        a8 = a_scr[pl.ds(start, SUBLANES), :]
        u8 = u_scr[pl.ds(start, SUBLANES), :]
        for d in (1, 2, 4):
            keep = row8 >= d
            u8 = jnp.where(keep, a8 * pltpu.roll(u8, d, 0) + u8, u8)
            a8 = jnp.where(keep, a8 * pltpu.roll(a8, d, 0), a8)
        h8 = u8 + a8 * hprev
        hs_scr[pl.ds(start, SUBLANES), :] = h8
        return h8[SUBLANES - 1:SUBLANES]

    hlast = lax.fori_loop(0, ts // SUBLANES, scan_group, hprev_scr[...])
    hprev_scr[...] = hlast

    y = jnp.dot((gate * hs_scr[...]).astype(BF16), wout_ref[...], preferred_element_type=F32)
    out_ref[...] = x + _rms(y, gpost_ref[...])


def _gate_slabs(w_a, w_x):
    def dense(w):
        n, bw, _ = w.shape
        d = jnp.einsum("ncd,nm->ncmd", w, jnp.eye(n, dtype=w.dtype)).reshape(n * bw, n * bw)
        pad = D_RNN_PAD - n * bw
        return jnp.pad(d, ((0, pad), (0, pad)))

    da, dx = dense(w_a), dense(w_x)
    slabs = []
    for col0, ncols, row0, nrows in GATE_WINDOWS:
        pad = ((0, GATE_K - nrows), (0, 2 * LANES - ncols))
        slabs.append(jnp.concatenate(
            [jnp.pad(d[row0:row0 + nrows, col0:col0 + ncols], pad) for d in (da, dx)], axis=1))
    return jnp.stack(slabs).astype(BF16)


def _rglru(x, g_pre, g_post, w_in, conv_w, conv_b, w_a, b_a, w_x, b_x, lam, w_out):
    b, s, _ = x.shape
    pad = D_RNN_PAD - D_RNN
    padc = lambda a: jnp.pad(a, ((0, 0), (0, pad)))
    win = jnp.concatenate([padc(w_in[:, :D_RNN]), padc(w_in[:, D_RNN:])], axis=1).astype(BF16)
    cp = padc(jnp.concatenate([conv_w, conv_b[None], b_a[None], b_x[None], lam[None]], axis=0))
    wout = jnp.pad(w_out, ((0, pad), (0, 0))).astype(BF16)
    wg = _gate_slabs(w_a, w_x)
    xspec = pl.BlockSpec((None, RG_TILE, D_MODEL), lambda bi, t: (bi, t, 0))
    return pl.pallas_call(
        _rglru_kernel,
        grid=(b, s // RG_TILE),
        in_specs=[xspec, _const_spec((1, D_MODEL)), _const_spec((1, D_MODEL)),
                  _const_spec(win.shape), _const_spec(cp.shape), _const_spec(wg.shape),
                  _const_spec(wout.shape)],
        out_specs=xspec,
        out_shape=jax.ShapeDtypeStruct(x.shape, F32),
        scratch_shapes=[pltpu.VMEM((SUBLANES + RG_TILE, D_RNN_PAD), F32),
                        pltpu.VMEM((1, D_RNN_PAD), F32),
                        pltpu.VMEM((RG_TILE, D_RNN_PAD), F32),
                        pltpu.VMEM((RG_TILE, D_RNN_PAD), F32),
                        pltpu.VMEM((RG_TILE, D_RNN_PAD), F32)],
        compiler_params=pltpu.CompilerParams(
            dimension_semantics=("arbitrary", "arbitrary"), vmem_limit_bytes=VMEM_LIMIT),
        name="rglru_mixer",
    )(x, g_pre, g_post, win, cp, wg, wout)


def kernel(x, attn_w_qkv, attn_w_o, rg_w_in, rg_conv_w, rg_conv_b, rg_w_a, rg_b_a, rg_w_x, rg_b_x, rg_lambda, rg_w_out, ffn_w_up, ffn_conv_w, ffn_conv_b, ffn_w_down, mix_pre_g, mix_post_g, ffn_pre_g, ffn_post_g):
    b, s, d = x.shape
    t = b * s
    row_g = lambda g: g.reshape(1, d)

    q, k, v = _qkv(x.reshape(t, d), row_g(mix_pre_g[0]), attn_w_qkv[0].astype(BF16))
    o = _attention(q.reshape(b, s, d), k.reshape(b, s, d), v.reshape(b, s, d))
    x = _proj_res(o.reshape(t, d), attn_w_o[0].astype(BF16), x.reshape(t, d), row_g(mix_post_g[0]))
    x = x.reshape(b, s, d)
    x = _ffn(x, row_g(ffn_pre_g[0]), row_g(ffn_post_g[0]),
             ffn_w_up[0], ffn_conv_w[0], ffn_conv_b[0], ffn_w_down[0])

    x = _rglru(x, row_g(mix_pre_g[1]), row_g(mix_post_g[1]), rg_w_in[0], rg_conv_w[0],
               rg_conv_b[0], rg_w_a[0], rg_b_a[0], rg_w_x[0], rg_b_x[0], rg_lambda[0], rg_w_out[0])
    x = _ffn(x, row_g(ffn_pre_g[1]), row_g(ffn_post_g[1]),
             ffn_w_up[1], ffn_conv_w[1], ffn_conv_b[1], ffn_w_down[1])
    return x
```

```python
import functools

import jax
import jax.numpy as jnp
from jax import lax
from jax.experimental import pallas as pl
from jax.experimental.pallas import tpu as pltpu

D_MODEL = 1024
N_HEADS = 16
HEAD_DIM = 64
D_RNN = 1344
RG_BLOCKS = 16
RG_CONV_W = 4
RG_C = 8.0
D_FF = 2816
FFN_CONV_W = 3
NORM_EPS = 1e-6

LANES = 128
SUBLANES = 8
D_RNN_PAD = 1408
VMEM_LIMIT = 56 * 1024 * 1024

ROW_TILE = 512
ATTN_TQ = 256
ATTN_TK = 256
ATTN_PAIRS = 4
FFN_CHUNK = 256
FFN_TILE = 512
RG_TILE = 256
EXIT_LOG = -110.0

F32 = jnp.float32
BF16 = jnp.bfloat16


def _rms(xf, g):
    inv = lax.rsqrt(jnp.mean(xf * xf, axis=-1, keepdims=True) + NORM_EPS)
    return xf * inv * g


GELU_C0 = 0.7978845608028654
GELU_C1 = GELU_C0 * 0.044715


def _const_spec(shape):
    nd = len(shape)
    return pl.BlockSpec(shape, lambda *_: (0,) * nd, pipeline_mode=pl.Buffered(1))


def _qkv_kernel(x_ref, g_ref, w_ref, q_ref, k_ref, v_ref):
    h = _rms(x_ref[...], g_ref[...]).astype(BF16)
    for c, (o_ref, scale) in enumerate(((q_ref, HEAD_DIM ** -0.5), (k_ref, None), (v_ref, None))):
        y = jnp.dot(h, w_ref[:, c * D_MODEL:(c + 1) * D_MODEL], preferred_element_type=F32)
        if scale is not None:
            y = y * scale
        o_ref[...] = y.astype(BF16)


def _qkv(x, g, w):
    t = x.shape[0]
    row = pl.BlockSpec((ROW_TILE, D_MODEL), lambda i: (i, 0))
    return pl.pallas_call(
        _qkv_kernel,
        grid=(t // ROW_TILE,),
        in_specs=[row, _const_spec((1, D_MODEL)), _const_spec((D_MODEL, 3 * D_MODEL))],
        out_specs=[row, row, row],
        out_shape=[jax.ShapeDtypeStruct((t, D_MODEL), BF16)] * 3,
        compiler_params=pltpu.CompilerParams(
            dimension_semantics=("arbitrary",), vmem_limit_bytes=VMEM_LIMIT),
        name="qkv_proj",
    )(x, g, w)


def _attn_kernel(q_ref, k_ref, v_ref, tri_ref, o_ref, acc_ref, carry_ref):
    i = pl.program_id(2)
    lane = lax.broadcasted_iota(jnp.int32, (1, LANES), 1)
    first = lane < HEAD_DIM
    qs = []
    for p in range(ATTN_PAIRS):
        q2 = q_ref[:, p * LANES:(p + 1) * LANES]
        zero = jnp.zeros_like(q2)
        qs += [jnp.where(first, q2, zero), jnp.where(first, zero, q2)]
    tri = tri_ref[...]

    def step(j, diagonal):
        start = pl.multiple_of(j * ATTN_TK, ATTN_TK)
        if diagonal:
            r = lax.broadcasted_iota(jnp.int32, (ATTN_TQ, ATTN_TK), 0)
            c = lax.broadcasted_iota(jnp.int32, (ATTN_TQ, ATTN_TK), 1)
            causal = c < r
        carry_max = None
        for h in range(2 * ATTN_PAIRS):
            cols = slice((h // 2) * LANES, (h // 2 + 1) * LANES)
            kb = k_ref[pl.ds(start, ATTN_TK), cols]
            vb = v_ref[pl.ds(start, ATTN_TK), cols]
            z = lax.dot_general(qs[h], kb, (((1,), (1,)), ((), ())),
                                preferred_element_type=F32)
            l = jnp.log(1.0 + jnp.exp(-jnp.abs(z)))
            log_beta = jnp.minimum(z, 0.0) - l
            log_1mb = log_beta - z
            if diagonal:
                log_1mb = jnp.where(causal, log_1mb, 0.0)
            stick = jnp.dot(log_1mb.astype(BF16), tri, preferred_element_type=F32)
            rowsum = jnp.sum(log_1mb, axis=1, keepdims=True)
            if diagonal:
                w = jnp.where(causal, jnp.exp(log_beta + stick), 0.0)
                acc_ref[h] = jnp.dot(w.astype(BF16), vb, preferred_element_type=F32)
                carry = jnp.broadcast_to(rowsum, (ATTN_TQ, LANES))
            else:
                cprev = carry_ref[h]
                w = jnp.exp(log_beta + stick + jnp.concatenate([cprev, cprev], axis=1))
                acc_ref[h] += jnp.dot(w.astype(BF16), vb, preferred_element_type=F32)
                carry = cprev + rowsum
            carry_ref[h] = carry
            carry_max = carry if carry_max is None else jnp.maximum(carry_max, carry)
        return jnp.max(carry_max)

    def cond(state):
        j, carry_max = state
        return jnp.logical_and(j >= 0, carry_max > EXIT_LOG)

    def body(state):
        j, _ = state
        return j - 1, step(j, False)

    lax.while_loop(cond, body, (i - 1, step(i, True)))
    for p in range(ATTN_PAIRS):
        o_ref[:, p * LANES:(p + 1) * LANES] = jnp.where(
            first, acc_ref[2 * p], acc_ref[2 * p + 1]).astype(BF16)


def _attention(q, k, v):
    b, s, _ = q.shape
    jj = lax.broadcasted_iota(jnp.int32, (ATTN_TK, ATTN_TK), 0)
    ss = lax.broadcasted_iota(jnp.int32, (ATTN_TK, ATTN_TK), 1)
    tri = (jj > ss).astype(BF16)
    width = ATTN_PAIRS * LANES
    qo_spec = pl.BlockSpec((None, ATTN_TQ, width), lambda bi, hp, i: (bi, i, hp))
    kv_spec = pl.BlockSpec((None, s, width), lambda bi, hp, i: (bi, 0, hp))
    return pl.pallas_call(
        _attn_kernel,
        grid=(b, N_HEADS // (2 * ATTN_PAIRS), s // ATTN_TQ),
        in_specs=[qo_spec, kv_spec, kv_spec, _const_spec((ATTN_TK, ATTN_TK))],
        out_specs=qo_spec,
        out_shape=jax.ShapeDtypeStruct((b, s, D_MODEL), BF16),
        scratch_shapes=[pltpu.VMEM((2 * ATTN_PAIRS, ATTN_TQ, LANES), F32),
                        pltpu.VMEM((2 * ATTN_PAIRS, ATTN_TQ, LANES), F32)],
        compiler_params=pltpu.CompilerParams(
            dimension_semantics=("arbitrary", "arbitrary", "arbitrary"),
            vmem_limit_bytes=VMEM_LIMIT),
        name="stickbreak_attn",
    )(q, k, v, tri)


def _proj_res_kernel(o_ref, w_ref, x_ref, g_ref, out_ref):
    y = jnp.dot(o_ref[...], w_ref[...], preferred_element_type=F32)
    out_ref[...] = x_ref[...] + _rms(y, g_ref[...])


def _proj_res(o, w, x, g):
    t = x.shape[0]
    row = pl.BlockSpec((ROW_TILE, D_MODEL), lambda i: (i, 0))
    return pl.pallas_call(
        _proj_res_kernel,
        grid=(t // ROW_TILE,),
        in_specs=[row, _const_spec((D_MODEL, D_MODEL)), row, _const_spec((1, D_MODEL))],
        out_specs=row,
        out_shape=jax.ShapeDtypeStruct((t, D_MODEL), F32),
        compiler_params=pltpu.CompilerParams(
            dimension_semantics=("arbitrary",), vmem_limit_bytes=VMEM_LIMIT),
        name="attn_out_proj",
    )(o, w, x, g)


def _ffn_kernel(x_ref, gpre_ref, gpost_ref, wu_ref, cp_ref, wd_ref, out_ref,
                h_scr, acc_scr, halo_scr, u_scr):
    n_chunks = cp_ref.shape[0]
    ts = x_ref.shape[0]

    @pl.when(pl.program_id(1) == 0)
    def _():
        halo_scr[...] = jnp.zeros_like(halo_scr)

    x = x_ref[...]
    h_scr[...] = _rms(x, gpre_ref[...]).astype(BF16)
    acc_scr[...] = jnp.zeros_like(acc_scr)

    def up(c, slot):
        off = c * FFN_CHUNK
        if not isinstance(c, int):
            off = pl.multiple_of(off, FFN_CHUNK)
        h = h_scr[...]
        ug = jnp.dot(h, wu_ref[:, pl.ds(off, FFN_CHUNK)], preferred_element_type=F32)
        uv = jnp.dot(h, wu_ref[:, pl.ds(D_FF + off, FFN_CHUNK)], preferred_element_type=F32)
        us = u_scr.at[slot]
        us[0:SUBLANES] = halo_scr[c]
        us[SUBLANES:SUBLANES + ts, 0:FFN_CHUNK] = ug
        us[SUBLANES:SUBLANES + ts, FFN_CHUNK:2 * FFN_CHUNK] = uv
        halo_scr[c] = jnp.concatenate([ug[ts - SUBLANES:ts], uv[ts - SUBLANES:ts]], axis=1)

    def down(c, slot):
        us = u_scr.at[slot]
        cp = cp_ref[c]
        cu = (cp[2:3] * us[SUBLANES:SUBLANES + ts] + cp[1:2] * us[SUBLANES - 1:SUBLANES - 1 + ts]
              + cp[0:1] * us[SUBLANES - 2:SUBLANES - 2 + ts] + cp[3:4])
        g = cu[:, :FFN_CHUNK]
        half_val = cu[:, FFN_CHUNK:]
        th = jnp.tanh(g * (GELU_C0 + GELU_C1 * (g * g)))
        act = ((g * th + g) * half_val).astype(BF16)
        acc_scr[...] += jnp.dot(act, wd_ref[c], preferred_element_type=F32)

    assert n_chunks % 2 == 1
    up(0, 0)

    def pair(p, _):
        c = 2 * p
        up(c + 1, 1)
        down(c, 0)
        up(c + 2, 0)
        down(c + 1, 1)
        return 0

    lax.fori_loop(0, n_chunks // 2, pair, 0)
    down(n_chunks - 1, 0)
    out_ref[...] = x + _rms(acc_scr[...], gpost_ref[...])


def _ffn(x, g_pre, g_post, w_up, conv_w, conv_b, w_down):
    b, s, _ = x.shape
    nc = D_FF // FFN_CHUNK

    def split(a):
        lead = a.shape[:-1]
        a = a.reshape(lead + (2, nc, FFN_CHUNK))
        a = jnp.moveaxis(a, -2, 0)
        return a.reshape((nc,) + lead + (2 * FFN_CHUNK,))

    wu = w_up.astype(BF16)
    cp = jnp.concatenate(
        [conv_w, conv_b[None], jnp.zeros((SUBLANES - FFN_CONV_W - 1, 2 * D_FF), F32)], axis=0)
    cp = cp * jnp.where(jnp.arange(2 * D_FF) < D_FF, 1.0, 0.5).astype(F32)
    cp = split(cp)
    wd = w_down.astype(BF16).reshape(nc, FFN_CHUNK, D_MODEL)
    xspec = pl.BlockSpec((None, FFN_TILE, D_MODEL), lambda bi, t: (bi, t, 0))
    return pl.pallas_call(
        _ffn_kernel,
        grid=(b, s // FFN_TILE),
        in_specs=[xspec, _const_spec((1, D_MODEL)), _const_spec((1, D_MODEL)),
                  _const_spec(wu.shape), _const_spec(cp.shape), _const_spec(wd.shape)],
        out_specs=xspec,
        out_shape=jax.ShapeDtypeStruct(x.shape, F32),
        scratch_shapes=[pltpu.VMEM((FFN_TILE, D_MODEL), BF16),
                        pltpu.VMEM((FFN_TILE, D_MODEL), F32),
                        pltpu.VMEM((nc, SUBLANES, 2 * FFN_CHUNK), F32),
                        pltpu.VMEM((2, SUBLANES + FFN_TILE, 2 * FFN_CHUNK), F32)],
        compiler_params=pltpu.CompilerParams(
            dimension_semantics=("arbitrary", "arbitrary"), vmem_limit_bytes=VMEM_LIMIT),
        name="conv_ffn",
    )(x, g_pre, g_post, wu, cp, wd)


def _gate_windows():
    bw = D_RNN // RG_BLOCKS
    out = []
    for col0 in range(0, D_RNN_PAD, 2 * LANES):
        ncols = min(2 * LANES, D_RNN_PAD - col0)
        b_lo = col0 // bw
        b_hi = min((col0 + ncols - 1) // bw, RG_BLOCKS - 1)
        row0 = (bw * b_lo) // LANES * LANES
        row1 = min(-(-(bw * (b_hi + 1)) // LANES) * LANES, D_RNN_PAD)
        out.append((col0, ncols, row0, row1 - row0))
    return out


GATE_WINDOWS = _gate_windows()
GATE_K = max(w[3] for w in GATE_WINDOWS)


def _rglru_kernel(x_ref, gpre_ref, gpost_ref, win_ref, cp_ref, wg_ref, wout_ref,
                  out_ref, rb_scr, hprev_scr, a_scr, u_scr, hs_scr):
    ts = x_ref.shape[0]
    c_pad = D_RNN_PAD

    @pl.when(pl.program_id(1) == 0)
    def _():
        rb_scr[0:SUBLANES] = jnp.zeros((SUBLANES, c_pad), F32)
        hprev_scr[...] = jnp.zeros_like(hprev_scr)

    x = x_ref[...]
    h = _rms(x, gpre_ref[...]).astype(BF16)
    proj = jnp.dot(h, win_ref[...], preferred_element_type=F32)
    gb = proj[:, :c_pad]
    gate2 = gb * jnp.tanh(gb * (GELU_C0 + GELU_C1 * (gb * gb))) + gb
    rb = proj[:, c_pad:]

    cp = cp_ref[...]
    rb_scr[SUBLANES:SUBLANES + ts] = rb
    rec = cp[RG_CONV_W - 1:RG_CONV_W] * rb + cp[RG_CONV_W:RG_CONV_W + 1]
    for d in range(1, RG_CONV_W):
        rec = rec + cp[RG_CONV_W - 1 - d:RG_CONV_W - d] * rb_scr[SUBLANES - d:SUBLANES - d + ts]
    rb_scr[0:SUBLANES] = rb[ts - SUBLANES:ts]

    recb = rec.astype(BF16)
    ra, rx = [], []
    for n, (_, ncols, row0, nrows) in enumerate(GATE_WINDOWS):
        res = jnp.dot(recb[:, row0:row0 + nrows], wg_ref[n, 0:nrows, :],
                      preferred_element_type=F32)
        ra.append(res[:, 0:ncols])
        rx.append(res[:, 2 * LANES:2 * LANES + ncols])
    r_gate = jax.nn.sigmoid(jnp.concatenate(ra, axis=1) + cp[5:6])
    i_gate = jax.nn.sigmoid(jnp.concatenate(rx, axis=1) + cp[6:7])
    nlam = -cp[7:8]
    softplus = jnp.maximum(nlam, 0.0) + jnp.log1p(jnp.exp(-jnp.abs(nlam)))
    log_a = (-RG_C) * r_gate * softplus
    a = jnp.exp(log_a)
    mult = jnp.sqrt((1.0 + a * a) * jnp.tanh(-log_a))
    a_scr[...] = a
    u_scr[...] = mult * i_gate * rec

    row8 = lax.broadcasted_iota(jnp.int32, (SUBLANES, c_pad), 0)

    def scan_group(g, hprev):
        start = pl.multiple_of(g * SUBLANES, SUBLANES)
        a8 = a_scr[pl.ds(start, SUBLANES), :]
        u8 = u_scr[pl.ds(start, SUBLANES), :]
        for d in (1, 2, 4):
            keep = row8 >= d
            u8 = jnp.where(keep, a8 * pltpu.roll(u8, d, 0) + u8, u8)
            a8 = jnp.where(keep, a8 * pltpu.roll(a8, d, 0), a8)
        h8 = u8 + a8 * hprev
        hs_scr[pl.ds(start, SUBLANES), :] = h8
        return h8[SUBLANES - 1:SUBLANES]

    hlast = lax.fori_loop(0, ts // SUBLANES, scan_group, hprev_scr[...], unroll=True)
    hprev_scr[...] = hlast

    y = jnp.dot((gate2 * hs_scr[...]).astype(BF16), wout_ref[...], preferred_element_type=F32)
    out_ref[...] = x + _rms(y, gpost_ref[...])


def _gate_slabs(w_a, w_x):
    def dense(w):
        n, bw, _ = w.shape
        d = jnp.einsum("ncd,nm->ncmd", w, jnp.eye(n, dtype=w.dtype)).reshape(n * bw, n * bw)
        pad = D_RNN_PAD - n * bw
        return jnp.pad(d, ((0, pad), (0, pad)))

    da, dx = dense(w_a), dense(w_x)
    slabs = []
    for col0, ncols, row0, nrows in GATE_WINDOWS:
        pad = ((0, GATE_K - nrows), (0, 2 * LANES - ncols))
        slabs.append(jnp.concatenate(
            [jnp.pad(d[row0:row0 + nrows, col0:col0 + ncols], pad) for d in (da, dx)], axis=1))
    return jnp.stack(slabs).astype(BF16)


def _rglru(x, g_pre, g_post, w_in, conv_w, conv_b, w_a, b_a, w_x, b_x, lam, w_out):
    b, s, _ = x.shape
    pad = D_RNN_PAD - D_RNN
    padc = lambda a: jnp.pad(a, ((0, 0), (0, pad)))
    win = jnp.concatenate([padc(w_in[:, :D_RNN]), padc(w_in[:, D_RNN:])], axis=1).astype(BF16)
    cp = padc(jnp.concatenate([conv_w, conv_b[None], b_a[None], b_x[None], lam[None]], axis=0))
    wout = (0.5 * jnp.pad(w_out, ((0, pad), (0, 0)))).astype(BF16)
    wg = _gate_slabs(w_a, w_x)
    xspec = pl.BlockSpec((None, RG_TILE, D_MODEL), lambda bi, t: (bi, t, 0))
    return pl.pallas_call(
        _rglru_kernel,
        grid=(b, s // RG_TILE),
        in_specs=[xspec, _const_spec((1, D_MODEL)), _const_spec((1, D_MODEL)),
                  _const_spec(win.shape), _const_spec(cp.shape), _const_spec(wg.shape),
                  _const_spec(wout.shape)],
        out_specs=xspec,
        out_shape=jax.ShapeDtypeStruct(x.shape, F32),
        scratch_shapes=[pltpu.VMEM((SUBLANES + RG_TILE, D_RNN_PAD), F32),
                        pltpu.VMEM((1, D_RNN_PAD), F32),
                        pltpu.VMEM((RG_TILE, D_RNN_PAD), F32),
                        pltpu.VMEM((RG_TILE, D_RNN_PAD), F32),
                        pltpu.VMEM((RG_TILE, D_RNN_PAD), F32)],
        compiler_params=pltpu.CompilerParams(
            dimension_semantics=("arbitrary", "arbitrary"), vmem_limit_bytes=VMEM_LIMIT),
        name="rglru_mixer",
    )(x, g_pre, g_post, win, cp, wg, wout)


def kernel(x, attn_w_qkv, attn_w_o, rg_w_in, rg_conv_w, rg_conv_b, rg_w_a, rg_b_a, rg_w_x, rg_b_x, rg_lambda, rg_w_out, ffn_w_up, ffn_conv_w, ffn_conv_b, ffn_w_down, mix_pre_g, mix_post_g, ffn_pre_g, ffn_post_g):
    b, s, d = x.shape
    t = b * s
    row_g = lambda g: g.reshape(1, d)

    q, k, v = _qkv(x.reshape(t, d), row_g(mix_pre_g[0]), attn_w_qkv[0].astype(BF16))
    o = _attention(q.reshape(b, s, d), k.reshape(b, s, d), v.reshape(b, s, d))
    x = _proj_res(o.reshape(t, d), attn_w_o[0].astype(BF16), x.reshape(t, d), row_g(mix_post_g[0]))
    x = x.reshape(b, s, d)
    x = _ffn(x, row_g(ffn_pre_g[0]), row_g(ffn_post_g[0]),
             ffn_w_up[0], ffn_conv_w[0], ffn_conv_b[0], ffn_w_down[0])

    x = _rglru(x, row_g(mix_pre_g[1]), row_g(mix_post_g[1]), rg_w_in[0], rg_conv_w[0],
               rg_conv_b[0], rg_w_a[0], rg_b_a[0], rg_w_x[0], rg_b_x[0], rg_lambda[0], rg_w_out[0])
    x = _ffn(x, row_g(ffn_pre_g[1]), row_g(ffn_post_g[1]),
             ffn_w_up[1], ffn_conv_w[1], ffn_conv_b[1], ffn_w_down[1])
    return x
```

```python
import functools

import jax
import jax.numpy as jnp
from jax import lax
from jax.experimental import pallas as pl
from jax.experimental.pallas import tpu as pltpu

D_MODEL = 1024
N_HEADS = 16
HEAD_DIM = 64
D_RNN = 1344
RG_BLOCKS = 16
RG_CONV_W = 4
RG_C = 8.0
D_FF = 2816
FFN_CONV_W = 3
NORM_EPS = 1e-6

LANES = 128
SUBLANES = 8
D_RNN_PAD = 1408
VMEM_LIMIT = 56 * 1024 * 1024

ROW_TILE = 512
ATTN_TQ = 256
ATTN_TK = 256
ATTN_PAIRS = 4
FFN_CHUNK = 256
FFN_TILE = 512
RG_TILE = 256
EXIT_LOG = -110.0

F32 = jnp.float32
BF16 = jnp.bfloat16


def _rms(xf, g):
    inv = lax.rsqrt(jnp.mean(xf * xf, axis=-1, keepdims=True) + NORM_EPS)
    return xf * inv * g


GELU_C0 = 0.7978845608028654
GELU_C1 = GELU_C0 * 0.044715


def _const_spec(shape):
    nd = len(shape)
    return pl.BlockSpec(shape, lambda *_: (0,) * nd, pipeline_mode=pl.Buffered(1))


def _qkv_kernel(x_ref, g_ref, w_ref, q_ref, k_ref, v_ref):
    h = _rms(x_ref[...], g_ref[...]).astype(BF16)
    for c, (o_ref, scale) in enumerate(((q_ref, HEAD_DIM ** -0.5), (k_ref, None), (v_ref, None))):
        y = jnp.dot(h, w_ref[:, c * D_MODEL:(c + 1) * D_MODEL], preferred_element_type=F32)
        if scale is not None:
            y = y * scale
        o_ref[...] = y.astype(BF16)


def _qkv(x, g, w):
    t = x.shape[0]
    row = pl.BlockSpec((ROW_TILE, D_MODEL), lambda i: (i, 0))
    return pl.pallas_call(
        _qkv_kernel,
        grid=(t // ROW_TILE,),
        in_specs=[row, _const_spec((1, D_MODEL)), _const_spec((D_MODEL, 3 * D_MODEL))],
        out_specs=[row, row, row],
        out_shape=[jax.ShapeDtypeStruct((t, D_MODEL), BF16)] * 3,
        compiler_params=pltpu.CompilerParams(
            dimension_semantics=("arbitrary",), vmem_limit_bytes=VMEM_LIMIT),
        name="qkv_proj",
    )(x, g, w)


def _attn_kernel(q_ref, k_ref, v_ref, tri_ref, o_ref, acc_ref, carry_ref):
    i = pl.program_id(2)
    lane = lax.broadcasted_iota(jnp.int32, (1, LANES), 1)
    first = lane < HEAD_DIM
    qs = []
    for p in range(ATTN_PAIRS):
        q2 = q_ref[:, p * LANES:(p + 1) * LANES]
        zero = jnp.zeros_like(q2)
        qs += [jnp.where(first, q2, zero), jnp.where(first, zero, q2)]
    tri = tri_ref[...]

    def step(j, diagonal):
        start = pl.multiple_of(j * ATTN_TK, ATTN_TK)
        if diagonal:
            r = lax.broadcasted_iota(jnp.int32, (ATTN_TQ, ATTN_TK), 0)
            c = lax.broadcasted_iota(jnp.int32, (ATTN_TQ, ATTN_TK), 1)
            causal = c < r
        carry_max = None
        for h in range(2 * ATTN_PAIRS):
            cols = slice((h // 2) * LANES, (h // 2 + 1) * LANES)
            kb = k_ref[pl.ds(start, ATTN_TK), cols]
            vb = v_ref[pl.ds(start, ATTN_TK), cols]
            z = lax.dot_general(qs[h], kb, (((1,), (1,)), ((), ())),
                                preferred_element_type=F32)
            l = jnp.log(1.0 + jnp.exp(-jnp.abs(z)))
            log_beta = jnp.minimum(z, 0.0) - l
            log_1mb = log_beta - z
            if diagonal:
                log_1mb = jnp.where(causal, log_1mb, 0.0)
            stick = jnp.dot(log_1mb.astype(BF16), tri, preferred_element_type=F32)
            rowsum = jnp.sum(log_1mb, axis=1, keepdims=True)
            if diagonal:
                w = jnp.where(causal, jnp.exp(log_beta + stick), 0.0)
                acc_ref[h] = jnp.dot(w.astype(BF16), vb, preferred_element_type=F32)
                carry = jnp.broadcast_to(rowsum, (ATTN_TQ, LANES))
            else:
                cprev = carry_ref[h]
                w = jnp.exp(log_beta + stick
                            + jnp.concatenate([cprev] * (ATTN_TK // LANES), axis=1))
                acc_ref[h] += jnp.dot(w.astype(BF16), vb, preferred_element_type=F32)
                carry = cprev + rowsum
            carry_ref[h] = carry
            carry_max = carry if carry_max is None else jnp.maximum(carry_max, carry)
        return jnp.max(carry_max)

    def cond(state):
        j, carry_max = state
        return jnp.logical_and(j >= 0, carry_max > EXIT_LOG)

    def body(state):
        j, _ = state
        return j - 1, step(j, False)

    lax.while_loop(cond, body, (i - 1, step(i, True)))
    for p in range(ATTN_PAIRS):
        o_ref[:, p * LANES:(p + 1) * LANES] = jnp.where(
            first, acc_ref[2 * p], acc_ref[2 * p + 1]).astype(BF16)


def _attention(q, k, v):
    b, s, _ = q.shape
    jj = lax.broadcasted_iota(jnp.int32, (ATTN_TK, ATTN_TK), 0)
    ss = lax.broadcasted_iota(jnp.int32, (ATTN_TK, ATTN_TK), 1)
    tri = (jj > ss).astype(BF16)
    width = ATTN_PAIRS * LANES
    qo_spec = pl.BlockSpec((None, ATTN_TQ, width), lambda bi, hp, i: (bi, i, hp))
    kv_spec = pl.BlockSpec((None, s, width), lambda bi, hp, i: (bi, 0, hp))
    return pl.pallas_call(
        _attn_kernel,
        grid=(b, N_HEADS // (2 * ATTN_PAIRS), s // ATTN_TQ),
        in_specs=[qo_spec, kv_spec, kv_spec, _const_spec((ATTN_TK, ATTN_TK))],
        out_specs=qo_spec,
        out_shape=jax.ShapeDtypeStruct((b, s, D_MODEL), BF16),
        scratch_shapes=[pltpu.VMEM((2 * ATTN_PAIRS, ATTN_TQ, LANES), F32),
                        pltpu.VMEM((2 * ATTN_PAIRS, ATTN_TQ, LANES), F32)],
        compiler_params=pltpu.CompilerParams(
            dimension_semantics=("arbitrary", "arbitrary", "arbitrary"),
            vmem_limit_bytes=VMEM_LIMIT),
        name="stickbreak_attn",
    )(q, k, v, tri)


def _ffn_kernel(*refs, mixer_proj):
    if mixer_proj:
        o_ref, wo_ref, gmix_ref, *refs = refs
    (x_ref, gpre_ref, gpost_ref, wu_ref, cp_ref, wd_ref, out_ref,
     h_scr, acc_scr, halo_scr, u_scr) = refs
    n_chunks = cp_ref.shape[0]
    ts = x_ref.shape[0]

    @pl.when(pl.program_id(1) == 0)
    def _():
        halo_scr[...] = jnp.zeros_like(halo_scr)

    x = x_ref[...]
    if mixer_proj:
        y = jnp.dot(o_ref[...], wo_ref[...], preferred_element_type=F32)
        x = x + _rms(y, gmix_ref[...])
    out_ref[...] = x
    h_scr[...] = _rms(x, gpre_ref[...]).astype(BF16)
    acc_scr[...] = jnp.zeros_like(acc_scr)

    def up(c, slot):
        off = c * FFN_CHUNK
        if not isinstance(c, int):
            off = pl.multiple_of(off, FFN_CHUNK)
        h = h_scr[...]
        ug = jnp.dot(h, wu_ref[:, pl.ds(off, FFN_CHUNK)], preferred_element_type=F32)
        uv = jnp.dot(h, wu_ref[:, pl.ds(D_FF + off, FFN_CHUNK)], preferred_element_type=F32)
        us = u_scr.at[slot]
        us[0:SUBLANES] = halo_scr[c]
        us[SUBLANES:SUBLANES + ts, 0:FFN_CHUNK] = ug
        us[SUBLANES:SUBLANES + ts, FFN_CHUNK:2 * FFN_CHUNK] = uv
        halo_scr[c] = jnp.concatenate([ug[ts - SUBLANES:ts], uv[ts - SUBLANES:ts]], axis=1)

    def down(c, slot):
        us = u_scr.at[slot]
        cp = cp_ref[c]
        cu = (cp[2:3] * us[SUBLANES:SUBLANES + ts] + cp[1:2] * us[SUBLANES - 1:SUBLANES - 1 + ts]
              + cp[0:1] * us[SUBLANES - 2:SUBLANES - 2 + ts] + cp[3:4])
        g = cu[:, :FFN_CHUNK]
        half_val = cu[:, FFN_CHUNK:]
        th = jnp.tanh(g * (GELU_C0 + GELU_C1 * (g * g)))
        act = ((g * th + g) * half_val).astype(BF16)
        acc_scr[...] += jnp.dot(act, wd_ref[c], preferred_element_type=F32)

    assert n_chunks % 2 == 1
    up(0, 0)

    def pair(p, _):
        c = 2 * p
        up(c + 1, 1)
        down(c, 0)
        up(c + 2, 0)
        down(c + 1, 1)
        return 0

    lax.fori_loop(0, n_chunks // 2, pair, 0)
    down(n_chunks - 1, 0)
    out_ref[...] = out_ref[...] + _rms(acc_scr[...], gpost_ref[...])


def _ffn(x, g_pre, g_post, w_up, conv_w, conv_b, w_down, mixer_proj=None):
    b, s, _ = x.shape
    nc = D_FF // FFN_CHUNK

    def split(a):
        lead = a.shape[:-1]
        a = a.reshape(lead + (2, nc, FFN_CHUNK))
        a = jnp.moveaxis(a, -2, 0)
        return a.reshape((nc,) + lead + (2 * FFN_CHUNK,))

    wu = w_up.astype(BF16)
    cp = jnp.concatenate(
        [conv_w, conv_b[None], jnp.zeros((SUBLANES - FFN_CONV_W - 1, 2 * D_FF), F32)], axis=0)
    cp = cp * jnp.where(jnp.arange(2 * D_FF) < D_FF, 1.0, 0.5).astype(F32)
    cp = split(cp)
    wd = w_down.astype(BF16).reshape(nc, FFN_CHUNK, D_MODEL)
    xspec = pl.BlockSpec((None, FFN_TILE, D_MODEL), lambda bi, t: (bi, t, 0))
    args = (x, g_pre, g_post, wu, cp, wd)
    in_specs = [xspec, _const_spec((1, D_MODEL)), _const_spec((1, D_MODEL)),
                _const_spec(wu.shape), _const_spec(cp.shape), _const_spec(wd.shape)]
    if mixer_proj is not None:
        args = tuple(mixer_proj) + args
        in_specs = [xspec, _const_spec((D_MODEL, D_MODEL)), _const_spec((1, D_MODEL))] + in_specs
    return pl.pallas_call(
        functools.partial(_ffn_kernel, mixer_proj=mixer_proj is not None),
        grid=(b, s // FFN_TILE),
        in_specs=in_specs,
        out_specs=xspec,
        out_shape=jax.ShapeDtypeStruct(x.shape, F32),
        scratch_shapes=[pltpu.VMEM((FFN_TILE, D_MODEL), BF16),
                        pltpu.VMEM((FFN_TILE, D_MODEL), F32),
                        pltpu.VMEM((nc, SUBLANES, 2 * FFN_CHUNK), F32),
                        pltpu.VMEM((2, SUBLANES + FFN_TILE, 2 * FFN_CHUNK), F32)],
        compiler_params=pltpu.CompilerParams(
            dimension_semantics=("arbitrary", "arbitrary"), vmem_limit_bytes=VMEM_LIMIT),
        name="conv_ffn",
    )(*args)


def _gate_windows():
    bw = D_RNN // RG_BLOCKS
    out = []
    for col0 in range(0, D_RNN_PAD, 2 * LANES):
        ncols = min(2 * LANES, D_RNN_PAD - col0)
        b_lo = col0 // bw
        b_hi = min((col0 + ncols - 1) // bw, RG_BLOCKS - 1)
        row0 = (bw * b_lo) // LANES * LANES
        row1 = min(-(-(bw * (b_hi + 1)) // LANES) * LANES, D_RNN_PAD)
        out.append((col0, ncols, row0, row1 - row0))
    return out


GATE_WINDOWS = _gate_windows()
GATE_K = max(w[3] for w in GATE_WINDOWS)


def _rglru_kernel(x_ref, gpre_ref, gpost_ref, win_ref, cp_ref, wg_ref, wout_ref,
                  out_ref, rb_scr, hprev_scr, a_scr, u_scr, hs_scr):
    ts = x_ref.shape[0]
    c_pad = D_RNN_PAD

    @pl.when(pl.program_id(1) == 0)
    def _():
        rb_scr[0:SUBLANES] = jnp.zeros((SUBLANES, c_pad), F32)
        hprev_scr[...] = jnp.zeros_like(hprev_scr)

    x = x_ref[...]
    h = _rms(x, gpre_ref[...]).astype(BF16)
    proj = jnp.dot(h, win_ref[...], preferred_element_type=F32)
    gb = proj[:, :c_pad]
    gate2 = gb * jnp.tanh(gb * (GELU_C0 + GELU_C1 * (gb * gb))) + gb
    rb = proj[:, c_pad:]

    cp = cp_ref[...]
    rb_scr[SUBLANES:SUBLANES + ts] = rb
    rec = cp[RG_CONV_W - 1:RG_CONV_W] * rb + cp[RG_CONV_W:RG_CONV_W + 1]
    for d in range(1, RG_CONV_W):
        rec = rec + cp[RG_CONV_W - 1 - d:RG_CONV_W - d] * rb_scr[SUBLANES - d:SUBLANES - d + ts]
    rb_scr[0:SUBLANES] = rb[ts - SUBLANES:ts]

    recb = rec.astype(BF16)
    ra, rx = [], []
    for n, (_, ncols, row0, nrows) in enumerate(GATE_WINDOWS):
        res = jnp.dot(recb[:, row0:row0 + nrows], wg_ref[n, 0:nrows, :],
                      preferred_element_type=F32)
        ra.append(res[:, 0:ncols])
        rx.append(res[:, 2 * LANES:2 * LANES + ncols])
    r_gate2 = 1.0 + jnp.tanh(0.5 * (jnp.concatenate(ra, axis=1) + cp[5:6]))
    i_gate = 0.5 + 0.5 * jnp.tanh(0.5 * (jnp.concatenate(rx, axis=1) + cp[6:7]))
    nlam = -cp[7:8]
    softplus = jnp.maximum(nlam, 0.0) + jnp.log1p(jnp.exp(-jnp.abs(nlam)))
    log_a = r_gate2 * ((-0.5 * RG_C) * softplus)
    a = jnp.exp(log_a)
    mult = jnp.sqrt((1.0 + a * a) * jnp.tanh(-log_a))
    a_scr[...] = a
    u_scr[...] = mult * i_gate * rec

    row8 = lax.broadcasted_iota(jnp.int32, (SUBLANES, c_pad), 0)

    def scan_group(g, hprev):
        start = pl.multiple_of(g * SUBLANES, SUBLANES)
        a8 = a_scr[pl.ds(start, SUBLANES), :]
        u8 = u_scr[pl.ds(start, SUBLANES), :]
        for d in (1, 2, 4):
            keep = row8 >= d
            u8 = jnp.where(keep, a8 * pltpu.roll(u8, d, 0) + u8, u8)
            a8 = jnp.where(keep, a8 * pltpu.roll(a8, d, 0), a8)
        h8 = u8 + a8 * hprev
        hs_scr[pl.ds(start, SUBLANES), :] = h8
        return h8[SUBLANES - 1:SUBLANES]

    hlast = lax.fori_loop(0, ts // SUBLANES, scan_group, hprev_scr[...], unroll=True)
    hprev_scr[...] = hlast

    y = jnp.dot((gate2 * hs_scr[...]).astype(BF16), wout_ref[...], preferred_element_type=F32)
    out_ref[...] = x + _rms(y, gpost_ref[...])


def _gate_slabs(w_a, w_x):
    def dense(w):
        n, bw, _ = w.shape
        d = jnp.einsum("ncd,nm->ncmd", w, jnp.eye(n, dtype=w.dtype)).reshape(n * bw, n * bw)
        pad = D_RNN_PAD - n * bw
        return jnp.pad(d, ((0, pad), (0, pad)))

    da, dx = dense(w_a), dense(w_x)
    slabs = []
    for col0, ncols, row0, nrows in GATE_WINDOWS:
        pad = ((0, GATE_K - nrows), (0, 2 * LANES - ncols))
        slabs.append(jnp.concatenate(
            [jnp.pad(d[row0:row0 + nrows, col0:col0 + ncols], pad) for d in (da, dx)], axis=1))
    return jnp.stack(slabs).astype(BF16)


def _rglru(x, g_pre, g_post, w_in, conv_w, conv_b, w_a, b_a, w_x, b_x, lam, w_out):
    b, s, _ = x.shape
    pad = D_RNN_PAD - D_RNN
    padc = lambda a: jnp.pad(a, ((0, 0), (0, pad)))
    win = jnp.concatenate([padc(w_in[:, :D_RNN]), padc(w_in[:, D_RNN:])], axis=1).astype(BF16)
    cp = padc(jnp.concatenate([conv_w, conv_b[None], b_a[None], b_x[None], lam[None]], axis=0))
    wout = (0.5 * jnp.pad(w_out, ((0, pad), (0, 0)))).astype(BF16)
    wg = _gate_slabs(w_a, w_x)
    xspec = pl.BlockSpec((None, RG_TILE, D_MODEL), lambda bi, t: (bi, t, 0))
    return pl.pallas_call(
        _rglru_kernel,
        grid=(b, s // RG_TILE),
        in_specs=[xspec, _const_spec((1, D_MODEL)), _const_spec((1, D_MODEL)),
                  _const_spec(win.shape), _const_spec(cp.shape), _const_spec(wg.shape),
                  _const_spec(wout.shape)],
        out_specs=xspec,
        out_shape=jax.ShapeDtypeStruct(x.shape, F32),
        scratch_shapes=[pltpu.VMEM((SUBLANES + RG_TILE, D_RNN_PAD), F32),
                        pltpu.VMEM((1, D_RNN_PAD), F32),
                        pltpu.VMEM((RG_TILE, D_RNN_PAD), F32),
                        pltpu.VMEM((RG_TILE, D_RNN_PAD), F32),
                        pltpu.VMEM((RG_TILE, D_RNN_PAD), F32)],
        compiler_params=pltpu.CompilerParams(
            dimension_semantics=("arbitrary", "arbitrary"), vmem_limit_bytes=VMEM_LIMIT),
        name="rglru_mixer",
    )(x, g_pre, g_post, win, cp, wg, wout)


def kernel(x, attn_w_qkv, attn_w_o, rg_w_in, rg_conv_w, rg_conv_b, rg_w_a, rg_b_a, rg_w_x, rg_b_x, rg_lambda, rg_w_out, ffn_w_up, ffn_conv_w, ffn_conv_b, ffn_w_down, mix_pre_g, mix_post_g, ffn_pre_g, ffn_post_g):
    b, s, d = x.shape
    t = b * s
    row_g = lambda g: g.reshape(1, d)

    q, k, v = _qkv(x.reshape(t, d), row_g(mix_pre_g[0]), attn_w_qkv[0].astype(BF16))
    o = _attention(q.reshape(b, s, d), k.reshape(b, s, d), v.reshape(b, s, d))
    x = _ffn(x, row_g(ffn_pre_g[0]), row_g(ffn_post_g[0]),
             ffn_w_up[0], ffn_conv_w[0], ffn_conv_b[0], ffn_w_down[0],
             mixer_proj=(o, attn_w_o[0].astype(BF16), row_g(mix_post_g[0])))

    x = _rglru(x, row_g(mix_pre_g[1]), row_g(mix_post_g[1]), rg_w_in[0], rg_conv_w[0],
               rg_conv_b[0], rg_w_a[0], rg_b_a[0], rg_w_x[0], rg_b_x[0], rg_lambda[0], rg_w_out[0])
    x = _ffn(x, row_g(ffn_pre_g[1]), row_g(ffn_post_g[1]),
             ffn_w_up[1], ffn_conv_w[1], ffn_conv_b[1], ffn_w_down[1])
    return x
```
